```python
import jax, jax.numpy as jnp
from jax import lax
import numpy as np

D_MODEL = 1024
BATCH = 8
SEQ = 4096
DEPTH = 4

CTX_LEN = 256
GRID_W = 64

GLA_HEADS = 4
GLA_KEY = D_MODEL // 2
GLA_VAL = D_MODEL
GLA_DK = GLA_KEY // GLA_HEADS
GLA_DV = GLA_VAL // GLA_HEADS
GLA_GATE_RANK = 16
GLA_TAU = 16.0
GLA_CHUNK = 64

MLA_HEADS = D_MODEL // 128
MLA_Q_RANK = 3 * D_MODEL // 8
MLA_KV_RANK = D_MODEL // 4
MLA_NOPE = 128
MLA_ROPE = 64
MLA_V = 128
MLA_SCALE = (MLA_NOPE + MLA_ROPE) ** -0.5
ROPE_F = MLA_ROPE // 4
ROPE_BASE = 10000.0
Q_BLOCK = 128

CONV_CH = D_MODEL
CONV_WIDTH = 31

D_FF = 128 * ((8 * D_MODEL // 3 + 127) // 128)
FFN_CONV_WIDTH = 3

DN_ALPHA = (2.0 * DEPTH) ** 0.25
DN_BETA = (8.0 * DEPTH) ** -0.25
NORM_EPS = 1e-6

IN_SPLITS = (GLA_KEY, GLA_KEY, GLA_VAL, GLA_VAL, GLA_GATE_RANK, GLA_GATE_RANK,
             MLA_Q_RANK, MLA_KV_RANK, MLA_ROPE, 2 * CONV_CH, 3 * D_MODEL)
N_IN = sum(IN_SPLITS)

kernel_name = 'hybrid_gla_mla_conformer_dit_trunk'


def _layer_norm(x, g, b):
    xf = x.astype(jnp.float32)
    mu = jnp.mean(xf, axis=-1, keepdims=True)
    var = jnp.mean(jnp.square(xf - mu), axis=-1, keepdims=True)
    return ((xf - mu) * lax.rsqrt(var + NORM_EPS) * g + b).astype(x.dtype)


def _rms_norm(x, g):
    xf = x.astype(jnp.float32)
    y = xf * lax.rsqrt(jnp.mean(xf * xf, axis=-1, keepdims=True) + NORM_EPS)
    return (y * g).astype(x.dtype)


def _split_cols(u):
    idx = []
    s = 0
    for n in IN_SPLITS[:-1]:
        s += n
        idx.append(s)
    return jnp.split(u, idx, axis=-1)


def _heads(t, n_heads):
    b, t_len, w = t.shape
    return t.reshape(b, t_len, n_heads, w // n_heads).transpose(0, 2, 1, 3)


def _merge_heads(t):
    b, h, t_len, d = t.shape
    return t.transpose(0, 2, 1, 3).reshape(b, t_len, h * d)


def _depthwise_conv(x, w, b):
    k, ch = w.shape
    pad = (k - 1) // 2
    y = lax.conv_general_dilated(x, w[:, None, :], window_strides=(1,), padding=[(pad, pad)],
                                 dimension_numbers=('NWC', 'WIO', 'NWC'), feature_group_count=ch)
    return y + b


def _axial_rope_tables(rows, dtype):
    row = jnp.repeat(jnp.arange(rows, dtype=jnp.float32), GRID_W)
    col = jnp.tile(jnp.arange(GRID_W, dtype=jnp.float32), rows)
    inv = ROPE_BASE ** (-2.0 * jnp.arange(ROPE_F, dtype=jnp.float32) / (MLA_ROPE // 2))
    ang = jnp.stack([row[:, None] * inv, col[:, None] * inv], axis=1)
    return jnp.cos(ang).astype(dtype), jnp.sin(ang).astype(dtype)


def _apply_axial_rope(x, cos, sin):
    shp = x.shape
    xr = x.reshape(shp[:-1] + (2, 2, ROPE_F))
    x0, x1 = xr[..., 0, :], xr[..., 1, :]
    out = jnp.stack([x0 * cos - x1 * sin, x0 * sin + x1 * cos], axis=-2)
    return out.reshape(shp)


def _gla_chunk_scan(q, k, v, log_a, state0):
    b, h, t_len, _ = q.shape
    n = t_len // GLA_CHUNK
    mask = jnp.tril(jnp.ones((GLA_CHUNK, GLA_CHUNK), dtype=bool))[:, :, None]

    def to_chunks(t):
        return jnp.moveaxis(t.reshape(b, h, n, GLA_CHUNK, t.shape[-1]), 2, 0)

    def step(state, inp):
        qc, kc, vc, gc = inp
        qf, kf, vf = qc.astype(jnp.float32), kc.astype(jnp.float32), vc.astype(jnp.float32)
        cum = jnp.cumsum(gc.astype(jnp.float32), axis=-2)
        o_inter = jnp.einsum('bhcd,bhde->bhce', qf * jnp.exp(cum), state)
        diff = cum[:, :, :, None, :] - cum[:, :, None, :, :]
        decay = jnp.exp(jnp.where(mask, diff, -jnp.inf))
        att = jnp.einsum('bhid,bhjd,bhijd->bhij', qf, kf, decay)
        o = o_inter + jnp.einsum('bhij,bhje->bhie', att, vf)
        last = cum[:, :, -1:, :]
        new_state = jnp.exp(last[:, :, 0, :])[..., None] * state + jnp.einsum(
            'bhcd,bhce->bhde', kf * jnp.exp(last - cum), vf)
        return new_state, o

    state, o = lax.scan(step, state0, (to_chunks(q), to_chunks(k), to_chunks(v), to_chunks(log_a)))
    o = jnp.moveaxis(o, 0, 2).reshape(b, h, t_len, v.shape[-1])
    return o, state


def _gla_bidirectional(q, k, v, log_f, log_b, s_f0, s_b0):
    o_f, s_f = _gla_chunk_scan(q, k, v, log_f, s_f0)
    flip = lambda t: jnp.flip(t, axis=2)
    o_b, s_b = _gla_chunk_scan(flip(q), flip(k), flip(v), flip(log_b), s_b0)
    return o_f + flip(o_b), s_f, s_b


def _gla_prep(pt, p):
    q = _heads(pt[0], GLA_HEADS) * (GLA_DK ** -0.5)
    k = _heads(pt[1], GLA_HEADS)
    v = _heads(pt[2], GLA_HEADS)
    log_f = _heads(jax.nn.log_sigmoid((pt[4] @ p['gla_wa_f'] + p['gla_ba_f']).astype(jnp.float32)) / GLA_TAU, GLA_HEADS)
    log_b = _heads(jax.nn.log_sigmoid((pt[5] @ p['gla_wa_b'] + p['gla_ba_b']).astype(jnp.float32)) / GLA_TAU, GLA_HEADS)
    return q, k, v, log_f, log_b


def _gla_out(o, r, p):
    of = o.astype(jnp.float32)
    of = of * lax.rsqrt(jnp.mean(of * of, axis=-1, keepdims=True) + NORM_EPS)
    of = of * p['gla_norm_g'].reshape(GLA_HEADS, 1, GLA_DV)
    y = _merge_heads(of).astype(r.dtype) * jax.nn.silu(r)
    return y @ p['gla_wo']


def _gla_mixer(parts, cparts, p, last):
    bsz = cparts[0].shape[0]
    zero = jnp.zeros((bsz, GLA_HEADS, GLA_DK, GLA_DV), jnp.float32)
    o_c, s_f, s_b = _gla_bidirectional(*_gla_prep(cparts, p), zero, zero)
    o, _, _ = _gla_bidirectional(*_gla_prep(parts, p), s_f, s_b)
    y = _gla_out(o, parts[3], p)
    if last:
        return y, None
    return y, _gla_out(o_c, cparts[3], p)


def _attend(qn, qr, kn, kr, v):
    s = (jnp.einsum('bhqd,bhkd->bhqk', qn, kn, preferred_element_type=jnp.float32)
         + jnp.einsum('bhqd,bkd->bhqk', qr, kr, preferred_element_type=jnp.float32)) * MLA_SCALE
    pr = jax.nn.softmax(s, axis=-1)
    return jnp.einsum('bhqk,bhkd->bhqd', pr.astype(v.dtype), v)


def _mla_latent(qn, qr, kn, kr, v, kn_c, kr_c, v_c):
    b, h, t_len, _ = qn.shape
    nb = t_len // Q_BLOCK
    kn_all = jnp.concatenate([kn, kn_c], axis=2)
    kr_all = jnp.concatenate([kr, kr_c], axis=1)
    v_all = jnp.concatenate([v, v_c], axis=2)

    def blocks(t):
        return jnp.moveaxis(t.reshape(b, h, nb, Q_BLOCK, t.shape[-1]), 2, 0)

    o = lax.map(lambda qb: _attend(qb[0], qb[1], kn_all, kr_all, v_all), (blocks(qn), blocks(qr)))
    return jnp.moveaxis(o, 0, 2).reshape(b, h, t_len, MLA_V)


def _mla_queries(pt, p, cos, sin, rope):
    cq = _rms_norm(pt[6], p['mla_q_norm'])
    q = _heads(cq @ p['mla_wuq'], MLA_HEADS)
    qn, qr = q[..., :MLA_NOPE], q[..., MLA_NOPE:]
    if rope:
        qr = _apply_axial_rope(qr, cos, sin)
    return qn, qr


def _mla_keys(pt, p, cos, sin, rope):
    ckv = _rms_norm(pt[7], p['mla_kv_norm'])
    kv = _heads(ckv @ p['mla_wukv'], MLA_HEADS)
    kn, v = kv[..., :MLA_NOPE], kv[..., MLA_NOPE:]
    kr = pt[8]
    if rope:
        kr = _apply_axial_rope(kr, cos, sin)
    return kn, kr, v


def _mla_mixer(parts, cparts, p, cos, sin, last):
    kn_c, kr_c, v_c = _mla_keys(cparts, p, cos, sin, False)
    kn, kr, v = _mla_keys(parts, p, cos, sin, True)
    qn, qr = _mla_queries(parts, p, cos, sin, True)
    y = _merge_heads(_mla_latent(qn, qr, kn, kr, v, kn_c, kr_c, v_c)) @ p['mla_wo']
    if last:
        return y, None
    qn_c, qr_c = _mla_queries(cparts, p, cos, sin, False)
    return y, _merge_heads(_attend(qn_c, qr_c, kn_c, kr_c, v_c)) @ p['mla_wo']


def _conv_module(u, p):
    a, g = jnp.split(u, 2, axis=-1)
    h = a * jax.nn.sigmoid(g)
    h = _depthwise_conv(h, p['conv_dw'], p['conv_db'])
    h = jax.nn.silu(_layer_norm(h, p['conv_ln_g'], p['conv_ln_b']))
    return h @ p['conv_wo']


def _merge_branches(pt, ys, p):
    g_a, g_b, g_c = jnp.split(jax.nn.sigmoid(pt[10]), 3, axis=-1)
    return (g_a * ys[0] + g_b * ys[1] + g_c * ys[2]) @ p['w_out']


def _token_mixer(h, hc, p, cos, sin, last):
    parts = _split_cols(h @ p['w_in'] + p['b_in'])
    cparts = _split_cols(hc @ p['w_in'] + p['b_in'])
    y_a, yc_a = _gla_mixer(parts, cparts, p, last)
    y_b, yc_b = _mla_mixer(parts, cparts, p, cos, sin, last)
    y = _merge_branches(parts, (y_a, y_b, _conv_module(parts[9], p)), p)
    if last:
        return y, None
    yc = _merge_branches(cparts, (yc_a, yc_b, _conv_module(cparts[9], p)), p)
    return y, yc


def _conv_ffn(h, p):
    u = _depthwise_conv(h @ p['ffn_wup'], p['ffn_dw'], p['ffn_db'])
    g, v = jnp.split(u, 2, axis=-1)
    return (jax.nn.silu(g) * v) @ p['ffn_wdown']


def _trunk_layer(x, xc, mod, cmod, p, cos, sin, last):
    shift1, scale1, gate1, shift2, scale2, gate2 = [m[:, None, :] for m in jnp.split(mod, 6, axis=-1)]
    cshift1, cscale1, cgate1, cshift2, cscale2, cgate2 = jnp.split(cmod, 6, axis=-1)
    y, yc = _token_mixer(x * (1 + scale1) + shift1, xc * (1 + cscale1) + cshift1, p, cos, sin, last)
    x = _layer_norm(DN_ALPHA * x + gate1 * y, p['ln1_g'], p['ln1_b'])
    x = _layer_norm(DN_ALPHA * x + gate2 * _conv_ffn(x * (1 + scale2) + shift2, p), p['ln2_g'], p['ln2_b'])
    if last:
        return x, None
    xc = _layer_norm(DN_ALPHA * xc + cgate1 * yc, p['ln1_g'], p['ln1_b'])
    xc = _layer_norm(DN_ALPHA * xc + cgate2 * _conv_ffn(xc * (1 + cscale2) + cshift2, p), p['ln2_g'], p['ln2_b'])
    return x, xc


def setup_inputs(seed: int = 0) -> dict:
    key = jax.random.key(seed)
    ks = jax.random.split(key, 33)

    def nrm(i, shape, scale):
        return jax.random.normal(ks[i], shape, jnp.float32) * scale

    L = DEPTH
    return {
        'x': nrm(0, (BATCH, SEQ, D_MODEL), 1.0),
        'c': nrm(1, (BATCH, D_MODEL), 1.0),
        'ctx': nrm(2, (BATCH, CTX_LEN, D_MODEL), 1.0),
        'c_ctx': nrm(3, (D_MODEL,), 1.0),
        'w_ada': nrm(4, (L, D_MODEL, 6 * D_MODEL), 0.5 * D_MODEL ** -0.5),
        'b_ada': nrm(5, (L, 6 * D_MODEL), 0.02),
        'w_in': nrm(6, (L, D_MODEL, N_IN), D_MODEL ** -0.5),
        'b_in': nrm(7, (L, N_IN), 0.02),
        'gla_wa_f': nrm(8, (L, GLA_GATE_RANK, GLA_KEY), GLA_GATE_RANK ** -0.5),
        'gla_ba_f': nrm(9, (L, GLA_KEY), 0.02),
        'gla_wa_b': nrm(10, (L, GLA_GATE_RANK, GLA_KEY), GLA_GATE_RANK ** -0.5),
        'gla_ba_b': nrm(11, (L, GLA_KEY), 0.02),
        'gla_norm_g': 1.0 + nrm(12, (L, GLA_VAL), 0.02),
        'gla_wo': nrm(13, (L, GLA_VAL, D_MODEL), DN_BETA * GLA_VAL ** -0.5),
        'mla_q_norm': 1.0 + nrm(14, (L, MLA_Q_RANK), 0.02),
        'mla_kv_norm': 1.0 + nrm(15, (L, MLA_KV_RANK), 0.02),
        'mla_wuq': nrm(16, (L, MLA_Q_RANK, MLA_HEADS * (MLA_NOPE + MLA_ROPE)), MLA_Q_RANK ** -0.5),
        'mla_wukv': nrm(17, (L, MLA_KV_RANK, MLA_HEADS * (MLA_NOPE + MLA_V)), MLA_KV_RANK ** -0.5),
        'mla_wo': nrm(18, (L, MLA_HEADS * MLA_V, D_MODEL), DN_BETA * (MLA_HEADS * MLA_V) ** -0.5),
        'conv_dw': nrm(19, (L, CONV_WIDTH, CONV_CH), CONV_WIDTH ** -0.5),
        'conv_db': nrm(20, (L, CONV_CH), 0.02),
        'conv_ln_g': 1.0 + nrm(21, (L, CONV_CH), 0.02),
        'conv_ln_b': nrm(22, (L, CONV_CH), 0.02),
        'conv_wo': nrm(23, (L, CONV_CH, D_MODEL), DN_BETA * CONV_CH ** -0.5),
        'w_out': nrm(24, (L, D_MODEL, D_MODEL), DN_BETA * D_MODEL ** -0.5),
        'ln1_g': 1.0 + nrm(25, (L, D_MODEL), 0.02),
        'ln1_b': nrm(26, (L, D_MODEL), 0.02),
        'ffn_wup': nrm(27, (L, D_MODEL, 2 * D_FF), D_MODEL ** -0.5),
        'ffn_dw': nrm(28, (L, FFN_CONV_WIDTH, 2 * D_FF), FFN_CONV_WIDTH ** -0.5),
        'ffn_db': nrm(29, (L, 2 * D_FF), 0.02),
        'ffn_wdown': nrm(30, (L, D_FF, D_MODEL), DN_BETA * D_FF ** -0.5),
        'ln2_g': 1.0 + nrm(31, (L, D_MODEL), 0.02),
        'ln2_b': nrm(32, (L, D_MODEL), 0.02),
    }


def reference(x, c, ctx, c_ctx, w_ada, b_ada, w_in, b_in, gla_wa_f, gla_ba_f, gla_wa_b, gla_ba_b,
              gla_norm_g, gla_wo, mla_q_norm, mla_kv_norm, mla_wuq, mla_wukv, mla_wo, conv_dw, conv_db,
              conv_ln_g, conv_ln_b, conv_wo, w_out, ln1_g, ln1_b, ffn_wup, ffn_dw, ffn_db, ffn_wdown,
              ln2_g, ln2_b):
    rows = x.shape[1] // GRID_W
    cos, sin = _axial_rope_tables(rows, x.dtype)
    silu_c = jax.nn.silu(c)
    silu_cc = jax.nn.silu(c_ctx)
    xc = ctx
    for l in range(DEPTH):
        p = {
            'w_in': w_in[l], 'b_in': b_in[l],
            'gla_wa_f': gla_wa_f[l], 'gla_ba_f': gla_ba_f[l], 'gla_wa_b': gla_wa_b[l], 'gla_ba_b': gla_ba_b[l],
            'gla_norm_g': gla_norm_g[l], 'gla_wo': gla_wo[l],
            'mla_q_norm': mla_q_norm[l], 'mla_kv_norm': mla_kv_norm[l], 'mla_wuq': mla_wuq[l],
            'mla_wukv': mla_wukv[l], 'mla_wo': mla_wo[l],
            'conv_dw': conv_dw[l], 'conv_db': conv_db[l], 'conv_ln_g': conv_ln_g[l], 'conv_ln_b': conv_ln_b[l],
            'conv_wo': conv_wo[l], 'w_out': w_out[l], 'ln1_g': ln1_g[l], 'ln1_b': ln1_b[l],
            'ffn_wup': ffn_wup[l], 'ffn_dw': ffn_dw[l], 'ffn_db': ffn_db[l], 'ffn_wdown': ffn_wdown[l],
            'ln2_g': ln2_g[l], 'ln2_b': ln2_b[l],
        }
        mod = silu_c @ w_ada[l] + b_ada[l]
        cmod = silu_cc @ w_ada[l] + b_ada[l]
        x, xc = _trunk_layer(x, xc, mod, cmod, p, cos, sin, l == DEPTH - 1)
    return x
```

```python
import functools

import numpy as np
import jax
import jax.numpy as jnp
from jax import lax
from jax.experimental import pallas as pl
from jax.experimental.pallas import tpu as pltpu

F32 = jnp.float32
BF16 = jnp.bfloat16

D_MODEL = 1024
GRID_W = 64

GLA_HEADS = 4
GLA_DK = 128
GLA_DV = 256
GLA_RANK = 16
GLA_TAU = 16.0
GLA_CHUNK = 128
GLA_LEVELS = 7
GLA_PRE_ROWS = 256

MLA_HEADS = 8
MLA_Q_RANK = 384
MLA_KV_RANK = 256
MLA_NOPE = 128
MLA_ROPE = 64
MLA_V = 128
MLA_SCALE = (MLA_NOPE + MLA_ROPE) ** -0.5
ROPE_F = MLA_ROPE // 4
ROPE_BASE = 10000.0
HEAD_LANES = 128

CONV_W = 31
CONV_HALO = 16
CONV_ROWS = 32
CONV_LANES = 256

D_FF = 2816
FFN_CHUNK = 256
FFN_HALO = 16

NORM_EPS = 1e-6

V7X_VMEM_LIMIT_BYTES = 56 * 1024 * 1024


def _params(semantics):
    return pltpu.CompilerParams(dimension_semantics=semantics,
                                vmem_limit_bytes=V7X_VMEM_LIMIT_BYTES)


def _dot(a, b):
    return jnp.dot(a, b, preferred_element_type=F32)


def _dot_nt(a, b):
    return lax.dot_general(a, b, (((1,), (1,)), ((), ())), preferred_element_type=F32)


def _dot_tn(a, b):
    return lax.dot_general(a, b, (((0,), (0,)), ((), ())), preferred_element_type=F32)


def _split2(x):
    hi = x.astype(BF16)
    lo = (x - hi.astype(F32)).astype(BF16)
    return hi, lo


def _split3(x):
    p1 = x.astype(BF16)
    r1 = x - p1.astype(F32)
    p2 = r1.astype(BF16)
    p3 = (r1 - p2.astype(F32)).astype(BF16)
    return p1, p2, p3


def _layer_norm_rows(z, g, b):
    mu = jnp.mean(z, axis=-1, keepdims=True)
    zc = z - mu
    var = jnp.mean(zc * zc, axis=-1, keepdims=True)
    return zc * lax.rsqrt(var + NORM_EPS) * g + b


def _rms_rows(z, g):
    return z * lax.rsqrt(jnp.mean(z * z, axis=-1, keepdims=True) + NORM_EPS) * g


def _ada_body(c_ref, w_ref, b_ref, o_ref):
    c = c_ref[...]
    s = c * jax.nn.sigmoid(c)
    sh, sl = _split2(s)
    wh, wl = _split2(w_ref[...])
    o_ref[...] = _dot(sh, wh) + _dot(sh, wl) + _dot(sl, wh) + b_ref[...]


def _ada(c_rows, w_ada, b_ada):
    depth, k, n = w_ada.shape
    rows = c_rows.shape[0]
    tn = 512
    return pl.pallas_call(
        _ada_body,
        grid=(depth, n // tn),
        in_specs=[pl.BlockSpec((rows, k), lambda l, j: (0, 0)),
                  pl.BlockSpec((None, k, tn), lambda l, j: (l, 0, j)),
                  pl.BlockSpec((None, 1, tn), lambda l, j: (l, 0, j))],
        out_specs=pl.BlockSpec((None, rows, tn), lambda l, j: (l, 0, j)),
        out_shape=jax.ShapeDtypeStruct((depth, rows, n), F32),
        compiler_params=_params(("parallel", "parallel")),
        name="ada_mod",
    )(c_rows, w_ada, b_ada.reshape(depth, 1, n))


def _proj_body(x_ref, sc_ref, sh_ref, *rest, n_w, act):
    w_refs, b_refs = rest[:n_w], rest[n_w:2 * n_w]
    o_ref, h_ref = rest[2 * n_w], rest[2 * n_w + 1]

    @pl.when(pl.program_id(2) == 0)
    def _():
        x = x_ref[...].astype(F32)
        h_ref[...] = (x * (1.0 + sc_ref[...]) + sh_ref[...]).astype(BF16)

    h = h_ref[...]
    ys = [_dot(h, w[...]) + b[...] for w, b in zip(w_refs, b_refs)]
    if act == "glu":
        y = ys[0] * jax.nn.sigmoid(ys[1])
    elif act == "sigmoid":
        y = jax.nn.sigmoid(ys[0])
    else:
        y = ys[0]
    o_ref[...] = y.astype(o_ref.dtype)


def _mod_map(mod, n_grid):
    batched = mod.shape[0] > 1
    if n_grid == 3:
        return (lambda b, i, j: (b, 0, 0)) if batched else (lambda b, i, j: (0, 0, 0))
    return (lambda b, i: (b, 0, 0)) if batched else (lambda b, i: (0, 0, 0))


def _proj(x, scale, shift, ws, bs, *, act, out_dtype, tn, name):
    bsz, t, k = x.shape
    n = ws[0].shape[1]
    tm = min(t, 1024)
    mmap = _mod_map(scale, 3)
    in_specs = [pl.BlockSpec((None, tm, k), lambda b, i, j: (b, i, 0)),
                pl.BlockSpec((None, 1, k), mmap),
                pl.BlockSpec((None, 1, k), mmap)]
    in_specs += [pl.BlockSpec((k, tn), lambda b, i, j: (0, j)) for _ in ws]
    in_specs += [pl.BlockSpec((1, tn), lambda b, i, j: (0, j)) for _ in bs]
    return pl.pallas_call(
        functools.partial(_proj_body, n_w=len(ws), act=act),
        grid=(bsz, t // tm, n // tn),
        in_specs=in_specs,
        out_specs=pl.BlockSpec((None, tm, tn), lambda b, i, j: (b, i, j)),
        out_shape=jax.ShapeDtypeStruct((bsz, t, n), out_dtype),
        scratch_shapes=[pltpu.VMEM((tm, k), BF16)],
        compiler_params=_params(("parallel", "parallel", "arbitrary")),
        name=name,
    )(x, scale, shift, *ws, *bs)


def _rope(x, cs, sup, sdn):
    return x * cs + pltpu.roll(x, HEAD_LANES - ROPE_F, 1) * sup + pltpu.roll(x, ROPE_F, 1) * sdn


def _mla_proj_body(x_ref, sc_ref, sh_ref, wcq, bcq, wckv, bckv, wkr, bkr, gq, gkv,
                   wuqn, wuqr, wukn, wuvt, cs_ref, sup_ref, sdn_ref, *outs, rope, want_q):
    x = x_ref[...].astype(F32)
    h = (x * (1.0 + sc_ref[...]) + sh_ref[...]).astype(BF16)
    if want_q:
        qn_o, qr_o, kn_o, kr_o, vt_o = outs
    else:
        kn_o, kr_o, vt_o = outs

    ckv = _rms_rows(_dot(h, wckv[...]) + bckv[...], gkv[...]).astype(BF16)
    kn_o[...] = _dot(ckv, wukn[...]).astype(BF16)
    vt_o[...] = _dot_nt(wuvt[...], ckv).astype(BF16)
    kr = _dot(h, wkr[...]) + bkr[...]
    if rope:
        kr = _rope(kr, cs_ref[...], sup_ref[...], sdn_ref[...])
    kr_o[...] = kr.astype(BF16)

    if want_q:
        cq = _rms_rows(_dot(h, wcq[...]) + bcq[...], gq[...]).astype(BF16)
        qn_o[...] = (_dot(cq, wuqn[...]) * MLA_SCALE).astype(BF16)
        qr = _dot(cq, wuqr[...]) * MLA_SCALE
        for hd in range(MLA_HEADS):
            sl = slice(hd * HEAD_LANES, (hd + 1) * HEAD_LANES)
            qh = qr[:, sl]
            if rope:
                qh = _rope(qh, cs_ref[...], sup_ref[...], sdn_ref[...])
            qr_o[:, sl] = qh.astype(BF16)


def _mla_proj(x, scale, shift, w, tabs, *, rope, want_q, tm, name):
    bsz, t, k = x.shape
    mmap = _mod_map(scale, 2)
    full = lambda a: pl.BlockSpec(a.shape, lambda b, i: (0,) * a.ndim)
    weights = [w["wcq"], w["bcq"], w["wckv"], w["bckv"], w["wkr"], w["bkr"], w["gq"], w["gkv"],
               w["wuqn"], w["wuqr"], w["wukn"], w["wuvt"]]
    hv = MLA_HEADS * HEAD_LANES
    row_spec = lambda width: pl.BlockSpec((None, tm, width), lambda b, i: (b, i, 0))
    out_specs, out_shape = [], []
    if want_q:
        out_specs += [row_spec(hv), row_spec(hv)]
        out_shape += [jax.ShapeDtypeStruct((bsz, t, hv), BF16)] * 2
    out_specs += [row_spec(hv), row_spec(HEAD_LANES),
                  pl.BlockSpec((None, None, hv, tm), lambda b, i: (b, i, 0, 0))]
    out_shape += [jax.ShapeDtypeStruct((bsz, t, hv), BF16),
                  jax.ShapeDtypeStruct((bsz, t, HEAD_LANES), BF16),
                  jax.ShapeDtypeStruct((bsz, t // tm, hv, tm), BF16)]
    tab_spec = pl.BlockSpec((tm, HEAD_LANES), lambda b, i: (i, 0))
    return pl.pallas_call(
        functools.partial(_mla_proj_body, rope=rope, want_q=want_q),
        grid=(bsz, t // tm),
        in_specs=[pl.BlockSpec((None, tm, k), lambda b, i: (b, i, 0)),
                  pl.BlockSpec((None, 1, k), mmap), pl.BlockSpec((None, 1, k), mmap)]
                 + [full(a) for a in weights] + [tab_spec] * 3,
        out_specs=out_specs,
        out_shape=out_shape,
        compiler_params=_params(("parallel", "parallel")),
        name=name,
    )(x, scale, shift, *weights, *tabs)


def _attn_body(qn_ref, qr_ref, *rest, chunks):
    o_ref = rest[-1]
    q = jnp.concatenate([qn_ref[...], qr_ref[...]], axis=-1)
    tq = q.shape[0]
    m = jnp.full((1, tq), -jnp.inf, F32)
    l = jnp.zeros((1, tq), F32)
    acc = jnp.zeros((MLA_V, tq), F32)

    for src, (n_chunks, tk) in enumerate(chunks):
        kn_ref, kr_ref, vt_ref = rest[3 * src:3 * src + 3]

        def step(c, carry, kn_ref=kn_ref, kr_ref=kr_ref, vt_ref=vt_ref, tk=tk):
            m, l, acc = carry
            rows = pl.ds(pl.multiple_of(c * tk, tk), tk)
            kc = jnp.concatenate([kn_ref[rows, :], kr_ref[rows, :]], axis=-1)
            s = _dot_nt(kc, q)
            m_new = jnp.maximum(m, jnp.max(s, axis=0, keepdims=True))
            p = jnp.exp(s - m_new)
            alpha = jnp.exp(m - m_new)
            l = alpha * l + jnp.sum(p, axis=0, keepdims=True)
            acc = alpha * acc + _dot(vt_ref[c], p.astype(BF16))
            return m_new, l, acc

        m, l, acc = lax.fori_loop(0, n_chunks, step, (m, l, acc))

    o_ref[...] = (acc * (1.0 / l)).T.astype(o_ref.dtype)


def _attn(qn, qr, sources, *, tq, name):
    bsz, t, _ = qn.shape
    in_specs = [pl.BlockSpec((None, tq, HEAD_LANES), lambda b, h, i: (b, i, h))] * 2
    args, chunks = [qn, qr], []
    for kn, kr, vt in sources:
        tk_total, n_chunks, tk = kn.shape[1], vt.shape[1], vt.shape[3]
        in_specs += [pl.BlockSpec((None, tk_total, HEAD_LANES), lambda b, h, i: (b, 0, h)),
                     pl.BlockSpec((None, tk_total, HEAD_LANES), lambda b, h, i: (b, 0, 0)),
                     pl.BlockSpec((None, n_chunks, MLA_V, tk), lambda b, h, i: (b, 0, h, 0))]
        args += [kn, kr, vt]
        chunks.append((n_chunks, tk))
    return pl.pallas_call(
        functools.partial(_attn_body, chunks=tuple(chunks)),
        grid=(bsz, MLA_HEADS, t // tq),
        in_specs=in_specs,
        out_specs=pl.BlockSpec((None, tq, MLA_V), lambda b, h, i: (b, i, h)),
        out_shape=jax.ShapeDtypeStruct((bsz, t, MLA_HEADS * MLA_V), BF16),
        compiler_params=_params(("parallel", "parallel", "parallel")),
        name=name,
    )(*args)


def _gla_tables():
    c = GLA_CHUNK
    i = np.arange(c)
    sgn = np.zeros((GLA_LEVELS, c, GLA_DK), np.float32)
    blk = np.zeros((GLA_LEVELS + 1, c, c), np.float32)
    for lv in range(GLA_LEVELS):
        s = c >> lv
        sgn[lv] = np.where((i % s) >= s // 2, 1.0, -1.0)[:, None]
        blk[lv] = (i[:, None] // s == i[None, :] // s)
    blk[GLA_LEVELS] = np.eye(c)
    r = np.arange(GLA_PRE_ROWS)
    same = r[:, None] // c == r[None, :] // c
    tri = np.stack([same & (r[None, :] <= r[:, None]), same & (r[None, :] >= r[:, None])])
    return jnp.asarray(sgn), jnp.asarray(blk), jnp.asarray(tri.astype(np.float32), dtype=BF16)


def _level_ref(cum, s, fwd):
    c = GLA_CHUNK
    off = s // 2 - 1 if fwd else s // 2
    if s >= 16:
        parts = [jnp.broadcast_to(cum[b * s + off:b * s + off + 1, :], (s, GLA_DK))
                 for b in range(c // s)]
        return parts[0] if len(parts) == 1 else jnp.concatenate(parts, axis=0)
    x3 = cum.reshape(c // 8, 8, GLA_DK)
    sub = lax.broadcasted_iota(jnp.int32, x3.shape, 1)
    out = None
    for b in range(8 // s):
        r = jnp.broadcast_to(x3[:, b * s + off:b * s + off + 1, :], x3.shape)
        out = r if out is None else jnp.where(sub >= b * s, r, out)
    return out.reshape(c, GLA_DK)


def _gla_chunk(q_ref, k_ref, v_ref, c_ref, st_ref, sgn_ref, blk_ref, r0, fwd):
    c = GLA_CHUNK
    rows = pl.ds(r0, c)
    cum = c_ref[rows, :]
    q = q_ref[rows, :].astype(F32) * (GLA_DK ** -0.5)
    k = k_ref[rows, :].astype(F32)
    v = v_ref[rows, :]
    att = _dot_nt(q.astype(BF16), k.astype(BF16)) * blk_ref[GLA_LEVELS]
    for lv in range(GLA_LEVELS):
        sg = sgn_ref[lv] if fwd else -sgn_ref[lv]
        e = jnp.exp(sg * (cum - _level_ref(cum, c >> lv, fwd)))
        eq = e * (0.5 * sg + 0.5)
        ek = e - eq
        att = att + blk_ref[lv] * _dot_nt((q * eq).astype(BF16), (k * ek).astype(BF16))
    last = cum[c - 1:c, :] if fwd else cum[0:1, :]
    st = st_ref[...]
    o = _dot_nt((q * jnp.exp(cum)).astype(BF16), st.astype(BF16))
    o = o + _dot(att.astype(BF16), v)
    kd = (k * jnp.exp(last - cum)).astype(BF16)
    st_ref[...] = st * jnp.exp(last) + _dot_tn(v, kd)
    return o


def _gla_body(q_ref, k_ref, v_ref, r_ref, g_ref, waf_ref, wab_ref, baf_ref, bab_ref, ng_ref,
              s0f_ref, s0b_ref, sgn_ref, blk_ref, tri_ref,
              y_ref, sf_ref, sb_ref,
              cf_ref, cb_ref, o_ref, stf_ref, stb_ref):
    t = q_ref.shape[0]
    c = GLA_CHUNK
    n = t // c

    def decay_rows(i, carry):
        rows = pl.ds(pl.multiple_of(i * GLA_PRE_ROWS, GLA_PRE_ROWS), GLA_PRE_ROWS)
        gh, gl = _split2(g_ref[rows, :])
        for w_ref, b_ref, out_ref, tri in ((waf_ref, baf_ref, cf_ref, tri_ref[0]),
                                           (wab_ref, bab_ref, cb_ref, tri_ref[1])):
            wh, wl = _split2(w_ref[...])
            z = _dot(gh, wh) + _dot(gh, wl) + _dot(gl, wh) + b_ref[...]
            logd = (jnp.minimum(z, 0.0) - jnp.log1p(jnp.exp(-jnp.abs(z)))) * (1.0 / GLA_TAU)
            p1, p2, p3 = _split3(logd)
            out_ref[rows, :] = _dot(tri, p1) + _dot(tri, p2) + _dot(tri, p3)
        return carry

    lax.fori_loop(0, t // GLA_PRE_ROWS, decay_rows, 0)

    stf_ref[...] = s0f_ref[...].T
    stb_ref[...] = s0b_ref[...].T

    def both(ci):
        rf = pl.multiple_of(ci * c, c)
        rb = pl.multiple_of((n - 1 - ci) * c, c)
        of = _gla_chunk(q_ref, k_ref, v_ref, cf_ref, stf_ref, sgn_ref, blk_ref, rf, True)
        ob = _gla_chunk(q_ref, k_ref, v_ref, cb_ref, stb_ref, sgn_ref, blk_ref, rb, False)
        return (rf, of), (rb, ob)

    def first_touch(ci, carry):
        for r0, o in both(ci):
            o_ref[pl.ds(r0, c), :] = o
        return carry

    def second_touch(ci, carry):
        for r0, o in both(ci):
            rows = pl.ds(r0, c)
            o = _rms_rows(o_ref[rows, :] + o, ng_ref[...])
            r = r_ref[rows, :].astype(F32)
            y_ref[rows, :] = (o * (r * jax.nn.sigmoid(r))).astype(y_ref.dtype)
        return carry

    lax.fori_loop(0, n // 2, first_touch, 0)
    lax.fori_loop(n // 2, n, second_touch, 0)

    sf_ref[...] = stf_ref[...].T
    sb_ref[...] = stb_ref[...].T


def _gla(qkvr, gates, w, s0f, s0b, tables, *, name):
    bsz, t, _ = qkvr.shape
    assert t % (2 * GLA_CHUNK) == 0 and t % GLA_PRE_ROWS == 0
    sgn, blk, tri = tables
    kv0 = GLA_HEADS * GLA_DK // GLA_DK
    v0 = 2 * GLA_HEADS * GLA_DK // GLA_DV
    r0 = v0 + GLA_HEADS
    col = lambda width, off: pl.BlockSpec((None, t, width), lambda b, h: (b, 0, off + h))
    head = lambda a: pl.BlockSpec((None,) + a.shape[1:], lambda b, h: (h,) + (0,) * (a.ndim - 1))
    full = lambda a: pl.BlockSpec(a.shape, lambda b, h: (0,) * a.ndim)
    state = pl.BlockSpec((None, None, GLA_DK, GLA_DV), lambda b, h: (b, h, 0, 0))
    return pl.pallas_call(
        _gla_body,
        grid=(bsz, GLA_HEADS),
        in_specs=[col(GLA_DK, 0), col(GLA_DK, kv0), col(GLA_DV, v0), col(GLA_DV, r0),
                  pl.BlockSpec((None, t, HEAD_LANES), lambda b, h: (b, 0, 0)),
                  head(w["waf"]), head(w["wab"]), head(w["baf"]), head(w["bab"]), head(w["ng"]),
                  state, state, full(sgn), full(blk), full(tri)],
        out_specs=[pl.BlockSpec((None, t, GLA_DV), lambda b, h: (b, 0, h)), state, state],
        out_shape=[jax.ShapeDtypeStruct((bsz, t, GLA_HEADS * GLA_DV), BF16),
                   jax.ShapeDtypeStruct((bsz, GLA_HEADS, GLA_DK, GLA_DV), F32),
                   jax.ShapeDtypeStruct((bsz, GLA_HEADS, GLA_DK, GLA_DV), F32)],
        scratch_shapes=[pltpu.VMEM((t, GLA_DK), F32), pltpu.VMEM((t, GLA_DK), F32),
                        pltpu.VMEM((t, GLA_DV), F32),
                        pltpu.VMEM((GLA_DV, GLA_DK), F32), pltpu.VMEM((GLA_DV, GLA_DK), F32)],
        compiler_params=_params(("parallel", "parallel")),
        name=name,
    )(qkvr, qkvr, qkvr, qkvr, gates, w["waf"], w["wab"], w["baf"], w["bab"], w["ng"],
      s0f, s0b, sgn, blk, tri)


def _conv_body(xm_ref, xp_ref, xn_ref, dw_ref, db_ref, lg_ref, lb_ref, o_ref, buf_ref, y_ref):
    tt = xm_ref.shape[0]
    i = pl.program_id(1)
    last = pl.num_programs(1) - 1
    keep_p = jnp.where(i == 0, 0.0, 1.0)
    keep_n = jnp.where(i == last, 0.0, 1.0)
    buf_ref[0:CONV_HALO, :] = xp_ref[...].astype(F32) * keep_p
    buf_ref[CONV_HALO:CONV_HALO + tt, :] = xm_ref[...].astype(F32)
    buf_ref[CONV_HALO + tt:, :] = xn_ref[...].astype(F32) * keep_n

    span = CONV_ROWS + 32

    def row_block(rb, carry):
        r0 = pl.multiple_of(rb * CONV_ROWS, CONV_ROWS)
        for cc in range(D_MODEL // CONV_LANES):
            lanes = slice(cc * CONV_LANES, (cc + 1) * CONV_LANES)
            acc = jnp.zeros((CONV_ROWS, CONV_LANES), F32) + db_ref[:, lanes]
            window = buf_ref[pl.ds(r0, span), lanes]
            for ph in range(8):
                xs = window if ph == 0 else pltpu.roll(window, span - ph, 0)
                for a in range(4):
                    tap = 8 * a + ph - 1
                    if 0 <= tap < CONV_W:
                        acc = acc + dw_ref[tap:tap + 1, lanes] * xs[8 * a:8 * a + CONV_ROWS, :]
            y_ref[pl.ds(r0, CONV_ROWS), lanes] = acc
        return carry

    lax.fori_loop(0, tt // CONV_ROWS, row_block, 0)
    z = _layer_norm_rows(y_ref[...], lg_ref[...], lb_ref[...])
    o_ref[...] = (z * jax.nn.sigmoid(z)).astype(o_ref.dtype)


def _conv(glu, dw, db, lg, lb, *, name):
    bsz, t, ch = glu.shape
    tt = min(t, 512)
    nh = tt // CONV_HALO
    n_halo = t // CONV_HALO
    full = lambda a: pl.BlockSpec(a.shape, lambda b, i: (0,) * a.ndim)
    return pl.pallas_call(
        _conv_body,
        grid=(bsz, t // tt),
        in_specs=[pl.BlockSpec((None, tt, ch), lambda b, i: (b, i, 0)),
                  pl.BlockSpec((None, CONV_HALO, ch),
                               lambda b, i: (b, jnp.maximum(i * nh - 1, 0), 0)),
                  pl.BlockSpec((None, CONV_HALO, ch),
                               lambda b, i: (b, jnp.minimum((i + 1) * nh, n_halo - 1), 0)),
                  full(dw), full(db), full(lg), full(lb)],
        out_specs=pl.BlockSpec((None, tt, ch), lambda b, i: (b, i, 0)),
        out_shape=jax.ShapeDtypeStruct((bsz, t, ch), BF16),
        scratch_shapes=[pltpu.VMEM((tt + 2 * CONV_HALO, ch), F32), pltpu.VMEM((tt, ch), F32)],
        compiler_params=_params(("parallel", "parallel")),
        name=name,
    )(glu, glu, glu, dw, db, lg, lb)


def _merge_body(pa_ref, pb_ref, pc_ref, gt_ref, x_ref, g1_ref, wa, wb, wc, wo, lg_ref, lb_ref,
                o_ref, *, alpha):
    d = D_MODEL
    m = gt_ref[:, 0:d].astype(F32) * _dot(pa_ref[...], wa[...])
    m = m + gt_ref[:, d:2 * d].astype(F32) * _dot(pb_ref[...], wb[...])
    m = m + gt_ref[:, 2 * d:3 * d].astype(F32) * _dot(pc_ref[...], wc[...])
    y = _dot(m.astype(BF16), wo[...])
    z = alpha * x_ref[...] + g1_ref[...] * y
    o_ref[...] = _layer_norm_rows(z, lg_ref[...], lb_ref[...])


def _merge(pa, pb, pc, gates, x, g1, w, *, alpha, name):
    bsz, t, d = x.shape
    tm = min(t, 512)
    row = lambda width: pl.BlockSpec((None, tm, width), lambda b, i: (b, i, 0))
    full = lambda a: pl.BlockSpec(a.shape, lambda b, i: (0,) * a.ndim)
    weights = [w["gla_wo"], w["mla_wo"], w["conv_wo"], w["w_out"], w["ln1_g"], w["ln1_b"]]
    return pl.pallas_call(
        functools.partial(_merge_body, alpha=alpha),
        grid=(bsz, t // tm),
        in_specs=[row(d), row(d), row(d), row(3 * d), row(d),
                  pl.BlockSpec((None, 1, d), _mod_map(g1, 2))] + [full(a) for a in weights],
        out_specs=row(d),
        out_shape=jax.ShapeDtypeStruct((bsz, t, d), F32),
        compiler_params=_params(("parallel", "parallel")),
        name=name,
    )(pa, pb, pc, gates, x, g1, *weights)


def _ffn_body(xm_ref, xp_ref, xn_ref, sc_ref, sh_ref, g2_ref, wg_ref, wv_ref,
              dwg_ref, dbg_ref, dwv_ref, dbv_ref, wd_ref, lg_ref, lb_ref,
              o_ref, h_ref, acc_ref, *, alpha):
    tm = xm_ref.shape[0]
    i, j = pl.program_id(1), pl.program_id(2)
    rows = tm + 2 * FFN_HALO

    @pl.when(j == 0)
    def _():
        mod = lambda x: x * (1.0 + sc_ref[...]) + sh_ref[...]
        keep_p = jnp.where(i == 0, 0.0, 1.0)
        keep_n = jnp.where(i == pl.num_programs(1) - 1, 0.0, 1.0)
        h_ref[0:FFN_HALO, :] = (mod(xp_ref[...]) * keep_p).astype(BF16)
        h_ref[FFN_HALO:FFN_HALO + tm, :] = mod(xm_ref[...]).astype(BF16)
        h_ref[FFN_HALO + tm:, :] = (mod(xn_ref[...]) * keep_n).astype(BF16)
        acc_ref[...] = jnp.zeros_like(acc_ref)

    h = h_ref[...]

    def conv3(u, dw_ref, db_ref):
        mid = slice(FFN_HALO, FFN_HALO + tm)
        up = pltpu.roll(u, 1, 0)[mid, :]
        un = pltpu.roll(u, rows - 1, 0)[mid, :]
        return dw_ref[0:1, :] * up + dw_ref[1:2, :] * u[mid, :] + dw_ref[2:3, :] * un + db_ref[...]

    cg = conv3(_dot(h, wg_ref[...]), dwg_ref, dbg_ref)
    cv = conv3(_dot(h, wv_ref[...]), dwv_ref, dbv_ref)
    a = (cg * jax.nn.sigmoid(cg) * cv).astype(BF16)
    acc_ref[...] += _dot(a, wd_ref[...])

    @pl.when(j == pl.num_programs(2) - 1)
    def _():
        z = alpha * xm_ref[...] + g2_ref[...] * acc_ref[...]
        o_ref[...] = _layer_norm_rows(z, lg_ref[...], lb_ref[...])


def _ffn(x, scale, shift, g2, w, *, alpha, name):
    bsz, t, d = x.shape
    tm = min(t, 1024)
    nh = tm // FFN_HALO
    n_halo = t // FFN_HALO
    nch = D_FF // FFN_CHUNK
    mmap = _mod_map(scale, 3)
    full = lambda a: pl.BlockSpec(a.shape, lambda b, i, j: (0,) * a.ndim)
    return pl.pallas_call(
        functools.partial(_ffn_body, alpha=alpha),
        grid=(bsz, t // tm, nch),
        in_specs=[pl.BlockSpec((None, tm, d), lambda b, i, j: (b, i, 0)),
                  pl.BlockSpec((None, FFN_HALO, d),
                               lambda b, i, j: (b, jnp.maximum(i * nh - 1, 0), 0)),
                  pl.BlockSpec((None, FFN_HALO, d),
                               lambda b, i, j: (b, jnp.minimum((i + 1) * nh, n_halo - 1), 0)),
                  pl.BlockSpec((None, 1, d), mmap), pl.BlockSpec((None, 1, d), mmap),
                  pl.BlockSpec((None, 1, d), mmap),
                  pl.BlockSpec((d, FFN_CHUNK), lambda b, i, j: (0, j)),
                  pl.BlockSpec((d, FFN_CHUNK), lambda b, i, j: (0, j + nch)),
                  pl.BlockSpec((3, FFN_CHUNK), lambda b, i, j: (0, j)),
                  pl.BlockSpec((1, FFN_CHUNK), lambda b, i, j: (0, j)),
                  pl.BlockSpec((3, FFN_CHUNK), lambda b, i, j: (0, j + nch)),
                  pl.BlockSpec((1, FFN_CHUNK), lambda b, i, j: (0, j + nch)),
                  pl.BlockSpec((FFN_CHUNK, d), lambda b, i, j: (j, 0)),
                  full(w["ln2_g"]), full(w["ln2_b"])],
        out_specs=pl.BlockSpec((None, tm, d), lambda b, i, j: (b, i, 0)),
        out_shape=jax.ShapeDtypeStruct((bsz, t, d), F32),
        scratch_shapes=[pltpu.VMEM((tm + 2 * FFN_HALO, d), BF16), pltpu.VMEM((tm, d), F32)],
        compiler_params=_params(("parallel", "parallel", "arbitrary")),
        name=name,
    )(x, x, x, scale, shift, g2, w["ffn_wup"], w["ffn_wup"], w["ffn_dw"], w["ffn_db"],
      w["ffn_dw"], w["ffn_db"], w["ffn_wdown"], w["ln2_g"], w["ln2_b"])


def _rope_tables(t):
    rows = t // GRID_W
    row = jnp.repeat(jnp.arange(rows, dtype=F32), GRID_W)
    colv = jnp.tile(jnp.arange(GRID_W, dtype=F32), rows)
    inv = ROPE_BASE ** (-2.0 * jnp.arange(ROPE_F, dtype=F32) / (MLA_ROPE // 2))
    ang = jnp.concatenate([row[:, None] * inv, row[:, None] * inv,
                           colv[:, None] * inv, colv[:, None] * inv], axis=1)
    cos, sin = jnp.cos(ang), jnp.sin(ang)
    lane = jnp.arange(MLA_ROPE)
    first = (lane % (2 * ROPE_F)) < ROPE_F
    pad = lambda a: jnp.pad(a, ((0, 0), (0, HEAD_LANES - MLA_ROPE)))
    return (pad(cos), pad(jnp.where(first, -sin, 0.0)), pad(jnp.where(first, 0.0, sin)))


def _prep_layer(p):
    d = D_MODEL
    w_in, b_in = p["w_in"], p["b_in"]
    edges = np.cumsum([0, 512, 512, 1024, 1024, 16, 16, MLA_Q_RANK, MLA_KV_RANK, MLA_ROPE, 2 * d, 3 * d])
    col = lambda a, b: (w_in[:, edges[a]:edges[b]].astype(BF16), b_in[edges[a]:edges[b]][None, :])
    out = {}
    out["w_gla"], out["b_gla"] = col(0, 4)
    wg, bg = col(4, 6)
    out["w_gate"] = jnp.pad(wg, ((0, 0), (0, HEAD_LANES - 2 * GLA_RANK)))
    out["b_gate"] = jnp.pad(bg, ((0, 0), (0, HEAD_LANES - 2 * GLA_RANK)))
    out["wcq"], out["bcq"] = col(6, 7)
    out["wckv"], out["bckv"] = col(7, 8)
    wkr, bkr = col(8, 9)
    out["wkr"] = jnp.pad(wkr, ((0, 0), (0, HEAD_LANES - MLA_ROPE)))
    out["bkr"] = jnp.pad(bkr, ((0, 0), (0, HEAD_LANES - MLA_ROPE)))
    wconv, bconv = col(9, 10)
    out["w_glu_a"], out["w_glu_g"] = wconv[:, :d], wconv[:, d:]
    out["b_glu_a"], out["b_glu_g"] = bconv[:, :d], bconv[:, d:]
    out["w_mg"], out["b_mg"] = col(10, 11)

    def gate_w(wa, lo):
        wa = wa.reshape(GLA_RANK, GLA_HEADS, GLA_DK).transpose(1, 0, 2)
        return jnp.pad(wa, ((0, 0), (lo, HEAD_LANES - GLA_RANK - lo), (0, 0)))
    out["waf"] = gate_w(p["gla_wa_f"], 0)
    out["wab"] = gate_w(p["gla_wa_b"], GLA_RANK)
    out["baf"] = p["gla_ba_f"].reshape(GLA_HEADS, 1, GLA_DK)
    out["bab"] = p["gla_ba_b"].reshape(GLA_HEADS, 1, GLA_DK)
    out["ng"] = p["gla_norm_g"].reshape(GLA_HEADS, 1, GLA_DV)

    out["gq"] = p["mla_q_norm"][None, :]
    out["gkv"] = p["mla_kv_norm"][None, :]
    wuq = p["mla_wuq"].reshape(MLA_Q_RANK, MLA_HEADS, MLA_NOPE + MLA_ROPE)
    out["wuqn"] = wuq[:, :, :MLA_NOPE].reshape(MLA_Q_RANK, -1).astype(BF16)
    out["wuqr"] = jnp.pad(wuq[:, :, MLA_NOPE:], ((0, 0), (0, 0), (0, HEAD_LANES - MLA_ROPE))
                          ).reshape(MLA_Q_RANK, -1).astype(BF16)
    wukv = p["mla_wukv"].reshape(MLA_KV_RANK, MLA_HEADS, MLA_NOPE + MLA_V)
    out["wukn"] = wukv[:, :, :MLA_NOPE].reshape(MLA_KV_RANK, -1).astype(BF16)
    out["wuvt"] = wukv[:, :, MLA_NOPE:].reshape(MLA_KV_RANK, -1).T.astype(BF16)

    for name in ("gla_wo", "mla_wo", "conv_wo", "w_out", "ffn_wup", "ffn_wdown"):
        out[name] = p[name].astype(BF16)
    for name in ("conv_db", "conv_ln_g", "conv_ln_b", "ln1_g", "ln1_b", "ffn_db", "ln2_g", "ln2_b"):
        out[name] = p[name][None, :]
    out["conv_dw"], out["ffn_dw"] = p["conv_dw"], p["ffn_dw"]
    return out


def _stream(x, mods, w, tabs, gla_tables, states, ctx_keys, *, rope, last_ctx, alpha, tag):
    sh1, sc1, g1, sh2, sc2, g2 = mods
    t = x.shape[1]
    tm = min(t, 512)
    qkvr = _proj(x, sc1, sh1, [w["w_gla"]], [w["b_gla"]], act="none", out_dtype=BF16,
                 tn=1024, name=f"gla_proj_{tag}")
    gates = _proj(x, sc1, sh1, [w["w_gate"]], [w["b_gate"]], act="none", out_dtype=F32,
                  tn=HEAD_LANES, name=f"gate_proj_{tag}")
    pre_a, sf, sb = _gla(qkvr, gates, w, states[0], states[1], gla_tables, name=f"gla_{tag}")
    mla = _mla_proj(x, sc1, sh1, w, tabs, rope=rope, want_q=not last_ctx, tm=tm,
                    name=f"mla_proj_{tag}")
    keys = tuple(mla[-3:])
    if last_ctx:
        return None, (sf, sb), keys
    qn, qr = mla[0], mla[1]
    pre_b = _attn(qn, qr, [keys] + ([ctx_keys] if ctx_keys is not None else []), tq=tm,
                  name=f"attn_{tag}")
    glu = _proj(x, sc1, sh1, [w["w_glu_a"], w["w_glu_g"]], [w["b_glu_a"], w["b_glu_g"]],
                act="glu", out_dtype=BF16, tn=512, name=f"glu_proj_{tag}")
    pre_c = _conv(glu, w["conv_dw"], w["conv_db"], w["conv_ln_g"], w["conv_ln_b"],
                  name=f"conv_{tag}")
    mg = _proj(x, sc1, sh1, [w["w_mg"]], [w["b_mg"]], act="sigmoid", out_dtype=BF16,
               tn=1024, name=f"merge_gate_{tag}")
    x1 = _merge(pre_a, pre_b, pre_c, mg, x, g1, w, alpha=alpha, name=f"merge_{tag}")
    x2 = _ffn(x1, sc2, sh2, g2, w, alpha=alpha, name=f"ffn_{tag}")
    return x2, (sf, sb), keys


def kernel(x, c, ctx, c_ctx, w_ada, b_ada, w_in, b_in, gla_wa_f, gla_ba_f, gla_wa_b, gla_ba_b,
           gla_norm_g, gla_wo, mla_q_norm, mla_kv_norm, mla_wuq, mla_wukv, mla_wo, conv_dw,
           conv_db, conv_ln_g, conv_ln_b, conv_wo, w_out, ln1_g, ln1_b, ffn_wup, ffn_dw, ffn_db,
           ffn_wdown, ln2_g, ln2_b):
    stacked = dict(w_in=w_in, b_in=b_in, gla_wa_f=gla_wa_f, gla_ba_f=gla_ba_f, gla_wa_b=gla_wa_b,
                   gla_ba_b=gla_ba_b, gla_norm_g=gla_norm_g, gla_wo=gla_wo, mla_q_norm=mla_q_norm,
                   mla_kv_norm=mla_kv_norm, mla_wuq=mla_wuq, mla_wukv=mla_wukv, mla_wo=mla_wo,
                   conv_dw=conv_dw, conv_db=conv_db, conv_ln_g=conv_ln_g, conv_ln_b=conv_ln_b,
                   conv_wo=conv_wo, w_out=w_out, ln1_g=ln1_g, ln1_b=ln1_b, ffn_wup=ffn_wup,
                   ffn_dw=ffn_dw, ffn_db=ffn_db, ffn_wdown=ffn_wdown, ln2_g=ln2_g, ln2_b=ln2_b)
    depth = w_in.shape[0]
    bsz, t, d = x.shape
    alpha = float((2.0 * depth) ** 0.25)

    mod_rows = 16
    c_rows = jnp.zeros((mod_rows, d), F32).at[:bsz].set(c).at[bsz].set(c_ctx)
    mods = _ada(c_rows, w_ada, b_ada)

    tabs = _rope_tables(t)
    ctx_tabs = tuple(a[:ctx.shape[1]] for a in tabs)
    gla_tables = _gla_tables()
    zero_state = jnp.zeros((bsz, GLA_HEADS, GLA_DK, GLA_DV), F32)

    xc = ctx
    for l in range(depth):
        w = _prep_layer({k: v[l] for k, v in stacked.items()})
        lat_mods = [m[:, None, :] for m in jnp.split(mods[l, :bsz], 6, axis=-1)]
        ctx_mods = [m[:, None, :] for m in jnp.split(mods[l, bsz:bsz + 1], 6, axis=-1)]
        last = l == depth - 1
        xc, ctx_states, ctx_keys = _stream(xc, ctx_mods, w, ctx_tabs, gla_tables,
                                           (zero_state, zero_state), None, rope=False,
                                           last_ctx=last, alpha=alpha, tag="ctx")
        x, _, _ = _stream(x, lat_mods, w, tabs, gla_tables, ctx_states, ctx_keys, rope=True,
                          last_ctx=False, alpha=alpha, tag="lat")
    return x
```

```python
import functools

import numpy as np
import jax
import jax.numpy as jnp
from jax import lax
from jax.experimental import pallas as pl
from jax.experimental.pallas import tpu as pltpu

F32 = jnp.float32
BF16 = jnp.bfloat16

D_MODEL = 1024
GRID_W = 64

GLA_HEADS = 4
GLA_DK = 128
GLA_DV = 256
GLA_RANK = 16
GLA_TAU = 16.0
GLA_CHUNK = 128
GLA_LEVELS = 7
LOG2_E = 1.4426950408889634

MLA_HEADS = 8
MLA_Q_RANK = 384
MLA_KV_RANK = 256
MLA_NOPE = 128
MLA_ROPE = 64
MLA_V = 128
MLA_SCALE = (MLA_NOPE + MLA_ROPE) ** -0.5
MLA_QSCALE = MLA_SCALE * LOG2_E
MLA_ONES_ROWS = 16
MLA_VT_ROWS = MLA_V + MLA_ONES_ROWS
ROPE_F = MLA_ROPE // 4
ROPE_BASE = 10000.0
HEAD_LANES = 128

CONV_W = 31
CONV_HALO = 16
CONV_ROWS = 32
CONV_LANES = 256

D_FF = 2816
FFN_CHUNK = 256
FFN_HALO = 16

NORM_EPS = 1e-6

V7X_VMEM_LIMIT_BYTES = 56 * 1024 * 1024


def _params(semantics):
    return pltpu.CompilerParams(dimension_semantics=semantics,
                                vmem_limit_bytes=V7X_VMEM_LIMIT_BYTES)


def _dot(a, b):
    return jnp.dot(a, b, preferred_element_type=F32)


def _dot_nt(a, b):
    return lax.dot_general(a, b, (((1,), (1,)), ((), ())), preferred_element_type=F32)


def _dot_tn(a, b):
    return lax.dot_general(a, b, (((0,), (0,)), ((), ())), preferred_element_type=F32)


def _split2(x):
    hi = x.astype(BF16)
    lo = (x - hi.astype(F32)).astype(BF16)
    return hi, lo


def _layer_norm_rows(z, g, b):
    mu = jnp.mean(z, axis=-1, keepdims=True)
    zc = z - mu
    var = jnp.mean(zc * zc, axis=-1, keepdims=True)
    return zc * lax.rsqrt(var + NORM_EPS) * g + b


def _rms_rows(z, g):
    return z * lax.rsqrt(jnp.mean(z * z, axis=-1, keepdims=True) + NORM_EPS) * g


def _ada_body(c_ref, w_ref, b_ref, o_ref):
    c = c_ref[...]
    s = c * jax.nn.sigmoid(c)
    sh, sl = _split2(s)
    wh, wl = _split2(w_ref[...])
    o_ref[...] = _dot(sh, wh) + _dot(sh, wl) + _dot(sl, wh) + b_ref[...]


def _ada(c_rows, w_ada, b_ada):
    depth, k, n = w_ada.shape
    rows = c_rows.shape[0]
    tn = 512
    return pl.pallas_call(
        _ada_body,
        grid=(depth, n // tn),
        in_specs=[pl.BlockSpec((rows, k), lambda l, j: (0, 0)),
                  pl.BlockSpec((None, k, tn), lambda l, j: (l, 0, j)),
                  pl.BlockSpec((None, 1, tn), lambda l, j: (l, 0, j))],
        out_specs=pl.BlockSpec((None, rows, tn), lambda l, j: (l, 0, j)),
        out_shape=jax.ShapeDtypeStruct((depth, rows, n), F32),
        compiler_params=_params(("parallel", "parallel")),
        name="ada_mod",
    )(c_rows, w_ada, b_ada.reshape(depth, 1, n))


def _proj_body(x_ref, sc_ref, sh_ref, *rest, n_w, act):
    w_refs, b_refs = rest[:n_w], rest[n_w:2 * n_w]
    o_ref, h_ref = rest[2 * n_w], rest[2 * n_w + 1]

    @pl.when(pl.program_id(2) == 0)
    def _():
        x = x_ref[...].astype(F32)
        h_ref[...] = (x * (1.0 + sc_ref[...]) + sh_ref[...]).astype(BF16)

    h = h_ref[...]
    ys = [_dot(h, w[...]) + b[...] for w, b in zip(w_refs, b_refs)]
    if act == "glu":
        y = ys[0] * jax.nn.sigmoid(ys[1])
    elif act == "sigmoid":
        y = jax.nn.sigmoid(ys[0])
    else:
        y = ys[0]
    o_ref[...] = y.astype(o_ref.dtype)


def _mod_map(mod, n_grid):
    batched = mod.shape[0] > 1
    if n_grid == 3:
        return (lambda b, i, j: (b, 0, 0)) if batched else (lambda b, i, j: (0, 0, 0))
    return (lambda b, i: (b, 0, 0)) if batched else (lambda b, i: (0, 0, 0))


def _proj(x, scale, shift, ws, bs, *, act, out_dtype, tn, name):
    bsz, t, k = x.shape
    n = ws[0].shape[1]
    tm = min(t, 1024)
    mmap = _mod_map(scale, 3)
    in_specs = [pl.BlockSpec((None, tm, k), lambda b, i, j: (b, i, 0)),
                pl.BlockSpec((None, 1, k), mmap),
                pl.BlockSpec((None, 1, k), mmap)]
    in_specs += [pl.BlockSpec((k, tn), lambda b, i, j: (0, j)) for _ in ws]
    in_specs += [pl.BlockSpec((1, tn), lambda b, i, j: (0, j)) for _ in bs]
    return pl.pallas_call(
        functools.partial(_proj_body, n_w=len(ws), act=act),
        grid=(bsz, t // tm, n // tn),
        in_specs=in_specs,
        out_specs=pl.BlockSpec((None, tm, tn), lambda b, i, j: (b, i, j)),
        out_shape=jax.ShapeDtypeStruct((bsz, t, n), out_dtype),
        scratch_shapes=[pltpu.VMEM((tm, k), BF16)],
        compiler_params=_params(("parallel", "parallel", "arbitrary")),
        name=name,
    )(x, scale, shift, *ws, *bs)


def _rope(x, cs, sup, sdn):
    return x * cs + pltpu.roll(x, HEAD_LANES - ROPE_F, 1) * sup + pltpu.roll(x, ROPE_F, 1) * sdn


def _mla_proj_body(x_ref, sc_ref, sh_ref, wcq, bcq, wckv, bckv, wkr, bkr, gq, gkv,
                   wuqn, wuqr, wukn, wuvt, cs_ref, sup_ref, sdn_ref, *outs, rope, want_q):
    x = x_ref[...].astype(F32)
    h = (x * (1.0 + sc_ref[...]) + sh_ref[...]).astype(BF16)
    if want_q:
        qn_o, qr_o, kn_o, kr_o, vt_o = outs
    else:
        kn_o, kr_o, vt_o = outs

    ckv = _rms_rows(_dot(h, wckv[...]) + bckv[...], gkv[...]).astype(BF16)
    kn_o[...] = _dot(ckv, wukn[...]).astype(BF16)
    vt = _dot_nt(wuvt[...], ckv).astype(BF16)
    ones = jnp.ones((MLA_ONES_ROWS, vt.shape[1]), BF16)
    for hd in range(MLA_HEADS):
        vt_o[hd * MLA_VT_ROWS:hd * MLA_VT_ROWS + MLA_V, :] = vt[hd * MLA_V:(hd + 1) * MLA_V, :]
        vt_o[hd * MLA_VT_ROWS + MLA_V:(hd + 1) * MLA_VT_ROWS, :] = ones
    kr = _dot(h, wkr[...]) + bkr[...]
    if rope:
        kr = _rope(kr, cs_ref[...], sup_ref[...], sdn_ref[...])
    kr_o[...] = kr.astype(BF16)

    if want_q:
        cq = _rms_rows(_dot(h, wcq[...]) + bcq[...], gq[...]).astype(BF16)
        qn_o[...] = (_dot(cq, wuqn[...]) * MLA_QSCALE).astype(BF16)
        qr = _dot(cq, wuqr[...]) * MLA_QSCALE
        for hd in range(MLA_HEADS):
            sl = slice(hd * HEAD_LANES, (hd + 1) * HEAD_LANES)
            qh = qr[:, sl]
            if rope:
                qh = _rope(qh, cs_ref[...], sup_ref[...], sdn_ref[...])
            qr_o[:, sl] = qh.astype(BF16)


def _mla_proj(x, scale, shift, w, tabs, *, rope, want_q, tm, name):
    bsz, t, k = x.shape
    mmap = _mod_map(scale, 2)
    full = lambda a: pl.BlockSpec(a.shape, lambda b, i: (0,) * a.ndim)
    weights = [w["wcq"], w["bcq"], w["wckv"], w["bckv"], w["wkr"], w["bkr"], w["gq"], w["gkv"],
               w["wuqn"], w["wuqr"], w["wukn"], w["wuvt"]]
    hv = MLA_HEADS * HEAD_LANES
    row_spec = lambda width: pl.BlockSpec((None, tm, width), lambda b, i: (b, i, 0))
    out_specs, out_shape = [], []
    if want_q:
        out_specs += [row_spec(hv), row_spec(hv)]
        out_shape += [jax.ShapeDtypeStruct((bsz, t, hv), BF16)] * 2
    out_specs += [row_spec(hv), row_spec(HEAD_LANES),
                  pl.BlockSpec((None, None, MLA_HEADS * MLA_VT_ROWS, tm), lambda b, i: (b, i, 0, 0))]
    out_shape += [jax.ShapeDtypeStruct((bsz, t, hv), BF16),
                  jax.ShapeDtypeStruct((bsz, t, HEAD_LANES), BF16),
                  jax.ShapeDtypeStruct((bsz, t // tm, MLA_HEADS * MLA_VT_ROWS, tm), BF16)]
    tab_spec = pl.BlockSpec((tm, HEAD_LANES), lambda b, i: (i, 0))
    return pl.pallas_call(
        functools.partial(_mla_proj_body, rope=rope, want_q=want_q),
        grid=(bsz, t // tm),
        in_specs=[pl.BlockSpec((None, tm, k), lambda b, i: (b, i, 0)),
                  pl.BlockSpec((None, 1, k), mmap), pl.BlockSpec((None, 1, k), mmap)]
                 + [full(a) for a in weights] + [tab_spec] * 3,
        out_specs=out_specs,
        out_shape=out_shape,
        compiler_params=_params(("parallel", "parallel")),
        name=name,
    )(x, scale, shift, *weights, *tabs)


def _attn_body(qn_ref, qr_ref, *rest, chunks):
    n_src = len(chunks)
    o_ref = rest[3 * n_src]
    kcat_refs = rest[3 * n_src + 1:]

    @pl.when(pl.program_id(2) == 0)
    def _():
        for src in range(n_src):
            kcat_refs[src][:, 0:HEAD_LANES] = rest[3 * src][...]
            kcat_refs[src][:, HEAD_LANES:] = rest[3 * src + 1][...]

    q = jnp.concatenate([qn_ref[...], qr_ref[...]], axis=-1)
    tq = q.shape[0]
    steps = [(src, c, tk) for src, (n_chunks, tk) in enumerate(chunks) for c in range(n_chunks)]
    scores = lambda src, c, tk: _dot_nt(kcat_refs[src][c * tk:(c + 1) * tk, :], q)

    m = jnp.full((1, tq), -jnp.inf, F32)
    acc = jnp.zeros((MLA_VT_ROWS, tq), F32)
    s_next = scores(*steps[0])
    for idx, (src, c, tk) in enumerate(steps):
        s = s_next
        if idx + 1 < len(steps):
            s_next = scores(*steps[idx + 1])
        m_new = jnp.maximum(m, jnp.max(s, axis=0, keepdims=True))
        p = jnp.exp2(s - m_new)
        acc = jnp.exp2(m - m_new) * acc + _dot(rest[3 * src + 2][c], p.astype(BF16))
        m = m_new

    l = acc[MLA_V:MLA_V + 1, :]
    o_ref[...] = (acc[0:MLA_V, :] * (1.0 / l)).T.astype(o_ref.dtype)


def _attn(qn, qr, sources, *, tq, name):
    bsz, t, _ = qn.shape
    in_specs = [pl.BlockSpec((None, tq, HEAD_LANES), lambda b, h, i: (b, i, h))] * 2
    args, chunks, scratch = [qn, qr], [], []
    for kn, kr, vt in sources:
        tk_total, n_chunks, tk = kn.shape[1], vt.shape[1], vt.shape[3]
        in_specs += [pl.BlockSpec((None, tk_total, HEAD_LANES), lambda b, h, i: (b, 0, h)),
                     pl.BlockSpec((None, tk_total, HEAD_LANES), lambda b, h, i: (b, 0, 0)),
                     pl.BlockSpec((None, n_chunks, MLA_VT_ROWS, tk), lambda b, h, i: (b, 0, h, 0))]
        args += [kn, kr, vt]
        chunks.append((n_chunks, tk))
        scratch.append(pltpu.VMEM((tk_total, 2 * HEAD_LANES), BF16))
    return pl.pallas_call(
        functools.partial(_attn_body, chunks=tuple(chunks)),
        grid=(bsz, MLA_HEADS, t // tq),
        in_specs=in_specs,
        out_specs=pl.BlockSpec((None, tq, MLA_V), lambda b, h, i: (b, i, h)),
        out_shape=jax.ShapeDtypeStruct((bsz, t, MLA_HEADS * MLA_V), BF16),
        scratch_shapes=scratch,
        compiler_params=_params(("parallel", "parallel", "arbitrary")),
        name=name,
    )(*args)


def _gla_tables():
    c = GLA_CHUNK
    i = np.arange(c)
    r, m = i[:, None], i[None, :]
    sums = np.zeros((2, GLA_LEVELS + 2, c, c), bool)
    pair = np.zeros((2, GLA_LEVELS + 1, c, c), bool)
    role = np.zeros((2, GLA_LEVELS, c), bool)
    sums[0, 0], sums[1, 0] = m <= r, m >= r
    sums[0, -1], sums[1, -1] = m > r, m < r
    for lv in range(GLA_LEVELS):
        s = c >> lv
        mid = ((i // s) * s + s // 2)[:, None]
        late = ((i % s) >= s // 2)
        same = (i[:, None] // s) == (i[None, :] // s)
        sums[0, 1 + lv] = np.where(late[:, None], (m >= mid) & (m <= r), (m > r) & (m < mid))
        sums[1, 1 + lv] = np.where(late[:, None], (m >= mid) & (m < r), (m >= r) & (m < mid))
        role[0, lv], role[1, lv] = late, ~late
        pair[0, lv] = same & late[:, None] & ~late[None, :]
        pair[1, lv] = same & ~late[:, None] & late[None, :]
    pair[:, GLA_LEVELS] = np.eye(c, dtype=bool)
    sums = sums.reshape(2, (GLA_LEVELS + 2) * c, c)
    sums = np.concatenate([sums, sums], axis=2)
    role = np.broadcast_to(role[..., None], role.shape + (GLA_DK,))
    return (jnp.asarray(sums.astype(np.float32), dtype=BF16), jnp.asarray(pair.astype(np.float32)),
            jnp.asarray(role.astype(np.float32), dtype=BF16))


def _gla_decay_steps(g_ref, gate_w, sums_ref, ex_ref, tot_ref, d, slot, r0):
    c = GLA_CHUNK
    wh, wl, bias = gate_w
    gh, gl = _split2(g_ref[pl.ds(r0, c), :])
    z = _dot(gh, wh) + _dot(gh, wl) + _dot(gl, wh) + bias
    yield
    logd = (jnp.minimum(z, 0.0) - jnp.log1p(jnp.exp(-jnp.abs(z)))) * (LOG2_E / GLA_TAU)
    p1, p2 = _split2(logd)
    parts = jnp.concatenate([p1, p2], axis=0)
    for n in range(GLA_LEVELS + 2):
        span = slice(n * c, (n + 1) * c)
        ex = jnp.exp2(_dot(sums_ref[d, span, :], parts))
        ex_ref[d, slot, span, :] = ex.astype(BF16)
        if n == 0:
            total = ex[c - 1:c, :] if d == 0 else ex[0:1, :]
            tot_ref[d, slot] = jnp.broadcast_to(total, tot_ref.shape[2:])
        yield


def _gla_chunk_steps(q_ref, k_ref, v_ref, pair_ref, role_ref, ex_ref, tot_ref, st_ref, d, slot, r0,
                     out):
    c = GLA_CHUNK
    fwd = d == 0
    rows = pl.ds(r0, c)
    block = lambda n: ex_ref[d, slot, n * c:(n + 1) * c, :]
    qb, kb, v = q_ref[rows, :], k_ref[rows, :], v_ref[rows, :]
    att = _dot_nt(qb, kb) * pair_ref[d, GLA_LEVELS]
    yield
    for lv in range(GLA_LEVELS):
        half = c >> (lv + 1)
        if half >= 16:
            x = jnp.concatenate([(qb if (b % 2 == 1) == fwd else kb)[b * half:(b + 1) * half, :]
                                 for b in range(c // half)], axis=0)
        else:
            x = jnp.where(role_ref[d, lv] > 0, qb, kb)
        x = x * block(1 + lv)
        att = att + pair_ref[d, lv] * _dot_nt(x, x)
        yield
    st = st_ref[...]
    o = _dot_nt(qb * block(0), st.astype(BF16)) + _dot(att.astype(BF16), v)
    yield
    st_ref[...] = st * tot_ref[d, slot, 0:1, :] + _dot_tn(v, kb * block(GLA_LEVELS + 1))
    out.append(o)
    yield


def _interleave(chains):
    chains = list(chains)
    while chains:
        for ch in list(chains):
            if next(ch, StopIteration) is StopIteration:
                chains.remove(ch)


def _gla_body(q_ref, k_ref, v_ref, r_ref, g_ref, waf_ref, wab_ref, baf_ref, bab_ref, ng_ref,
              s0f_ref, s0b_ref, sums_ref, pair_ref, role_ref,
              y_ref, sf_ref, sb_ref,
              o_ref, stf_ref, stb_ref, ex_ref, tot_ref):
    t = q_ref.shape[0]
    c = GLA_CHUNK
    n = t // c

    stf_ref[...] = s0f_ref[...].T
    stb_ref[...] = s0b_ref[...].T
    gates = (_split2(waf_ref[...]) + (baf_ref[...],), _split2(wab_ref[...]) + (bab_ref[...],))
    states = (stf_ref, stb_ref)
    starts = lambda ci: (pl.multiple_of(ci * c, c), pl.multiple_of((n - 1 - ci) * c, c))
    decay = lambda d, slot, r0: _gla_decay_steps(g_ref, gates[d], sums_ref, ex_ref, tot_ref,
                                                 d, slot, r0)

    _interleave(decay(d, 0, r0) for d, r0 in enumerate(starts(0)))

    def both(ci):
        slot = ci % 2
        nxt = starts(jnp.minimum(ci + 1, n - 1))
        out = ([], [])
        _interleave(
            [_gla_chunk_steps(q_ref, k_ref, v_ref, pair_ref, role_ref, ex_ref, tot_ref, states[d],
                              d, slot, r0, out[d]) for d, r0 in enumerate(starts(ci))]
            + [decay(d, 1 - slot, r0) for d, r0 in enumerate(nxt)])
        return [(r0, out[d][0]) for d, r0 in enumerate(starts(ci))]

    def first_touch(ci, carry):
        for r0, o in both(ci):
            o_ref[pl.ds(r0, c), :] = o
        return carry

    def second_touch(ci, carry):
        for r0, o in both(ci):
            rows = pl.ds(r0, c)
            o = _rms_rows(o_ref[rows, :] + o, ng_ref[...])
            r = r_ref[rows, :].astype(F32)
            y_ref[rows, :] = (o * (r * jax.nn.sigmoid(r))).astype(y_ref.dtype)
        return carry

    lax.fori_loop(0, n // 2, first_touch, 0)
    lax.fori_loop(n // 2, n, second_touch, 0)

    sf_ref[...] = stf_ref[...].T
    sb_ref[...] = stb_ref[...].T


def _gla(qkvr, gates, w, s0f, s0b, tables, *, name):
    bsz, t, _ = qkvr.shape
    assert t % (2 * GLA_CHUNK) == 0
    sums, pair, role = tables
    kv0 = GLA_HEADS * GLA_DK // GLA_DK
    v0 = 2 * GLA_HEADS * GLA_DK // GLA_DV
    r0 = v0 + GLA_HEADS
    col = lambda width, off: pl.BlockSpec((None, t, width), lambda b, h: (b, 0, off + h))
    head = lambda a: pl.BlockSpec((None,) + a.shape[1:], lambda b, h: (h,) + (0,) * (a.ndim - 1))
    full = lambda a: pl.BlockSpec(a.shape, lambda b, h: (0,) * a.ndim)
    state = pl.BlockSpec((None, None, GLA_DK, GLA_DV), lambda b, h: (b, h, 0, 0))
    return pl.pallas_call(
        _gla_body,
        grid=(bsz, GLA_HEADS),
        in_specs=[col(GLA_DK, 0), col(GLA_DK, kv0), col(GLA_DV, v0), col(GLA_DV, r0),
                  pl.BlockSpec((None, t, HEAD_LANES), lambda b, h: (b, 0, 0)),
                  head(w["waf"]), head(w["wab"]), head(w["baf"]), head(w["bab"]), head(w["ng"]),
                  state, state, full(sums), full(pair), full(role)],
        out_specs=[pl.BlockSpec((None, t, GLA_DV), lambda b, h: (b, 0, h)), state, state],
        out_shape=[jax.ShapeDtypeStruct((bsz, t, GLA_HEADS * GLA_DV), BF16),
                   jax.ShapeDtypeStruct((bsz, GLA_HEADS, GLA_DK, GLA_DV), F32),
                   jax.ShapeDtypeStruct((bsz, GLA_HEADS, GLA_DK, GLA_DV), F32)],
        scratch_shapes=[pltpu.VMEM((t, GLA_DV), F32),
                        pltpu.VMEM((GLA_DV, GLA_DK), F32), pltpu.VMEM((GLA_DV, GLA_DK), F32),
                        pltpu.VMEM((2, 2, (GLA_LEVELS + 2) * GLA_CHUNK, GLA_DK), BF16),
                        pltpu.VMEM((2, 2, 8, GLA_DK), F32)],
        compiler_params=_params(("parallel", "parallel")),
        name=name,
    )(qkvr, qkvr, qkvr, qkvr, gates, w["waf"], w["wab"], w["baf"], w["bab"], w["ng"],
      s0f, s0b, sums, pair, role)


def _conv_body(xm_ref, xp_ref, xn_ref, dw_ref, db_ref, lg_ref, lb_ref, o_ref, buf_ref, y_ref):
    tt = xm_ref.shape[0]
    i = pl.program_id(1)
    last = pl.num_programs(1) - 1
    keep_p = jnp.where(i == 0, 0.0, 1.0)
    keep_n = jnp.where(i == last, 0.0, 1.0)
    buf_ref[0:CONV_HALO, :] = xp_ref[...].astype(F32) * keep_p
    buf_ref[CONV_HALO:CONV_HALO + tt, :] = xm_ref[...].astype(F32)
    buf_ref[CONV_HALO + tt:, :] = xn_ref[...].astype(F32) * keep_n

    span = CONV_ROWS + 32

    def row_block(rb, carry):
        r0 = pl.multiple_of(rb * CONV_ROWS, CONV_ROWS)
        for cc in range(D_MODEL // CONV_LANES):
            lanes = slice(cc * CONV_LANES, (cc + 1) * CONV_LANES)
            acc = jnp.zeros((CONV_ROWS, CONV_LANES), F32) + db_ref[:, lanes]
            window = buf_ref[pl.ds(r0, span), lanes]
            for ph in range(8):
                xs = window if ph == 0 else pltpu.roll(window, span - ph, 0)
                for a in range(4):
                    tap = 8 * a + ph - 1
                    if 0 <= tap < CONV_W:
                        acc = acc + dw_ref[tap:tap + 1, lanes] * xs[8 * a:8 * a + CONV_ROWS, :]
            y_ref[pl.ds(r0, CONV_ROWS), lanes] = acc
        return carry

    lax.fori_loop(0, tt // CONV_ROWS, row_block, 0)
    z = _layer_norm_rows(y_ref[...], lg_ref[...], lb_ref[...])
    o_ref[...] = (z * jax.nn.sigmoid(z)).astype(o_ref.dtype)


def _conv(glu, dw, db, lg, lb, *, name):
    bsz, t, ch = glu.shape
    tt = min(t, 512)
    nh = tt // CONV_HALO
    n_halo = t // CONV_HALO
    full = lambda a: pl.BlockSpec(a.shape, lambda b, i: (0,) * a.ndim)
    return pl.pallas_call(
        _conv_body,
        grid=(bsz, t // tt),
        in_specs=[pl.BlockSpec((None, tt, ch), lambda b, i: (b, i, 0)),
                  pl.BlockSpec((None, CONV_HALO, ch),
                               lambda b, i: (b, jnp.maximum(i * nh - 1, 0), 0)),
                  pl.BlockSpec((None, CONV_HALO, ch),
                               lambda b, i: (b, jnp.minimum((i + 1) * nh, n_halo - 1), 0)),
                  full(dw), full(db), full(lg), full(lb)],
        out_specs=pl.BlockSpec((None, tt, ch), lambda b, i: (b, i, 0)),
        out_shape=jax.ShapeDtypeStruct((bsz, t, ch), BF16),
        scratch_shapes=[pltpu.VMEM((tt + 2 * CONV_HALO, ch), F32), pltpu.VMEM((tt, ch), F32)],
        compiler_params=_params(("parallel", "parallel")),
        name=name,
    )(glu, glu, glu, dw, db, lg, lb)


def _merge_body(pa_ref, pb_ref, pc_ref, gt_ref, x_ref, g1_ref, wa, wb, wc, wo, lg_ref, lb_ref,
                o_ref, *, alpha):
    d = D_MODEL
    m = gt_ref[:, 0:d].astype(F32) * _dot(pa_ref[...], wa[...])
    m = m + gt_ref[:, d:2 * d].astype(F32) * _dot(pb_ref[...], wb[...])
    m = m + gt_ref[:, 2 * d:3 * d].astype(F32) * _dot(pc_ref[...], wc[...])
    y = _dot(m.astype(BF16), wo[...])
    z = alpha * x_ref[...] + g1_ref[...] * y
    o_ref[...] = _layer_norm_rows(z, lg_ref[...], lb_ref[...])


def _merge(pa, pb, pc, gates, x, g1, w, *, alpha, name):
    bsz, t, d = x.shape
    tm = min(t, 512)
    row = lambda width: pl.BlockSpec((None, tm, width), lambda b, i: (b, i, 0))
    full = lambda a: pl.BlockSpec(a.shape, lambda b, i: (0,) * a.ndim,
                                  pipeline_mode=pl.Buffered(1))
    weights = [w["gla_wo"], w["mla_wo"], w["conv_wo"], w["w_out"], w["ln1_g"], w["ln1_b"]]
    return pl.pallas_call(
        functools.partial(_merge_body, alpha=alpha),
        grid=(bsz, t // tm),
        in_specs=[row(d), row(d), row(d), row(3 * d), row(d),
                  pl.BlockSpec((None, 1, d), _mod_map(g1, 2))] + [full(a) for a in weights],
        out_specs=row(d),
        out_shape=jax.ShapeDtypeStruct((bsz, t, d), F32),
        compiler_params=_params(("parallel", "parallel")),
        name=name,
    )(pa, pb, pc, gates, x, g1, *weights)


def _ffn_body(xm_ref, xp_ref, xn_ref, sc_ref, sh_ref, g2_ref, wup_ref, dw_ref, db_ref, wd_ref,
              lg_ref, lb_ref, o_ref, *, alpha):
    tm = xm_ref.shape[0]
    i = pl.program_id(1)
    rows = tm + 2 * FFN_HALO
    mid = slice(FFN_HALO, FFN_HALO + tm)

    mod = lambda x: x * (1.0 + sc_ref[...]) + sh_ref[...]
    keep_p = jnp.where(i == 0, 0.0, 1.0)
    keep_n = jnp.where(i == pl.num_programs(1) - 1, 0.0, 1.0)
    h = jnp.concatenate([(mod(xp_ref[...]) * keep_p).astype(BF16),
                         mod(xm_ref[...]).astype(BF16),
                         (mod(xn_ref[...]) * keep_n).astype(BF16)], axis=0)

    def conv3(col):
        u = _dot(h, wup_ref[:, col])
        up = pltpu.roll(u, 1, 0)[mid, :]
        un = pltpu.roll(u, rows - 1, 0)[mid, :]
        return (dw_ref[0:1, col] * up + dw_ref[1:2, col] * u[mid, :] + dw_ref[2:3, col] * un
                + db_ref[:, col])

    y = None
    for c in range(D_FF // FFN_CHUNK):
        cg = conv3(slice(c * FFN_CHUNK, (c + 1) * FFN_CHUNK))
        cv = conv3(slice(D_FF + c * FFN_CHUNK, D_FF + (c + 1) * FFN_CHUNK))
        a = (cg * jax.nn.sigmoid(cg) * cv).astype(BF16)
        part = _dot(a, wd_ref[c * FFN_CHUNK:(c + 1) * FFN_CHUNK, :])
        y = part if y is None else y + part

    z = alpha * xm_ref[...] + g2_ref[...] * y
    o_ref[...] = _layer_norm_rows(z, lg_ref[...], lb_ref[...])


def _ffn(x, scale, shift, g2, w, *, alpha, name):
    bsz, t, d = x.shape
    tm = min(t, 512)
    nh = tm // FFN_HALO
    n_halo = t // FFN_HALO
    mmap = _mod_map(scale, 2)
    resident = lambda a: pl.BlockSpec(a.shape, lambda b, i: (0,) * a.ndim,
                                      pipeline_mode=pl.Buffered(1))
    weights = [w["ffn_wup"], w["ffn_dw"], w["ffn_db"], w["ffn_wdown"], w["ln2_g"], w["ln2_b"]]
    return pl.pallas_call(
        functools.partial(_ffn_body, alpha=alpha),
        grid=(bsz, t // tm),
        in_specs=[pl.BlockSpec((None, tm, d), lambda b, i: (b, i, 0)),
                  pl.BlockSpec((None, FFN_HALO, d),
                               lambda b, i: (b, jnp.maximum(i * nh - 1, 0), 0)),
                  pl.BlockSpec((None, FFN_HALO, d),
                               lambda b, i: (b, jnp.minimum((i + 1) * nh, n_halo - 1), 0)),
                  pl.BlockSpec((None, 1, d), mmap), pl.BlockSpec((None, 1, d), mmap),
                  pl.BlockSpec((None, 1, d), mmap)] + [resident(a) for a in weights],
        out_specs=pl.BlockSpec((None, tm, d), lambda b, i: (b, i, 0)),
        out_shape=jax.ShapeDtypeStruct((bsz, t, d), F32),
        compiler_params=_params(("parallel", "parallel")),
        name=name,
    )(x, x, x, scale, shift, g2, *weights)


def _rope_tables(t):
    rows = t // GRID_W
    row = jnp.repeat(jnp.arange(rows, dtype=F32), GRID_W)
    colv = jnp.tile(jnp.arange(GRID_W, dtype=F32), rows)
    inv = ROPE_BASE ** (-2.0 * jnp.arange(ROPE_F, dtype=F32) / (MLA_ROPE // 2))
    ang = jnp.concatenate([row[:, None] * inv, row[:, None] * inv,
                           colv[:, None] * inv, colv[:, None] * inv], axis=1)
    cos, sin = jnp.cos(ang), jnp.sin(ang)
    lane = jnp.arange(MLA_ROPE)
    first = (lane % (2 * ROPE_F)) < ROPE_F
    pad = lambda a: jnp.pad(a, ((0, 0), (0, HEAD_LANES - MLA_ROPE)))
    return (pad(cos), pad(jnp.where(first, -sin, 0.0)), pad(jnp.where(first, 0.0, sin)))


def _prep_layer(p):
    d = D_MODEL
    w_in, b_in = p["w_in"], p["b_in"]
    edges = np.cumsum([0, 512, 512, 1024, 1024, 16, 16, MLA_Q_RANK, MLA_KV_RANK, MLA_ROPE, 2 * d, 3 * d])
    col = lambda a, b: (w_in[:, edges[a]:edges[b]].astype(BF16), b_in[edges[a]:edges[b]][None, :])
    out = {}
    qscale = jnp.where(jnp.arange(edges[4]) < edges[1], GLA_DK ** -0.5, 1.0)
    out["w_gla"] = (w_in[:, :edges[4]] * qscale).astype(BF16)
    out["b_gla"] = (b_in[:edges[4]] * qscale)[None, :]
    wg, bg = col(4, 6)
    out["w_gate"] = jnp.pad(wg, ((0, 0), (0, HEAD_LANES - 2 * GLA_RANK)))
    out["b_gate"] = jnp.pad(bg, ((0, 0), (0, HEAD_LANES - 2 * GLA_RANK)))
    out["wcq"], out["bcq"] = col(6, 7)
    out["wckv"], out["bckv"] = col(7, 8)
    wkr, bkr = col(8, 9)
    out["wkr"] = jnp.pad(wkr, ((0, 0), (0, HEAD_LANES - MLA_ROPE)))
    out["bkr"] = jnp.pad(bkr, ((0, 0), (0, HEAD_LANES - MLA_ROPE)))
    wconv, bconv = col(9, 10)
    out["w_glu_a"], out["w_glu_g"] = wconv[:, :d], wconv[:, d:]
    out["b_glu_a"], out["b_glu_g"] = bconv[:, :d], bconv[:, d:]
    out["w_mg"], out["b_mg"] = col(10, 11)

    def gate_w(wa, lo):
        wa = wa.reshape(GLA_RANK, GLA_HEADS, GLA_DK).transpose(1, 0, 2)
        return jnp.pad(wa, ((0, 0), (lo, HEAD_LANES - GLA_RANK - lo), (0, 0)))
    out["waf"] = gate_w(p["gla_wa_f"], 0)
    out["wab"] = gate_w(p["gla_wa_b"], GLA_RANK)
    out["baf"] = p["gla_ba_f"].reshape(GLA_HEADS, 1, GLA_DK)
    out["bab"] = p["gla_ba_b"].reshape(GLA_HEADS, 1, GLA_DK)
    out["ng"] = p["gla_norm_g"].reshape(GLA_HEADS, 1, GLA_DV)

    out["gq"] = p["mla_q_norm"][None, :]
    out["gkv"] = p["mla_kv_norm"][None, :]
    wuq = p["mla_wuq"].reshape(MLA_Q_RANK, MLA_HEADS, MLA_NOPE + MLA_ROPE)
    out["wuqn"] = wuq[:, :, :MLA_NOPE].reshape(MLA_Q_RANK, -1).astype(BF16)
    out["wuqr"] = jnp.pad(wuq[:, :, MLA_NOPE:], ((0, 0), (0, 0), (0, HEAD_LANES - MLA_ROPE))
                          ).reshape(MLA_Q_RANK, -1).astype(BF16)
    wukv = p["mla_wukv"].reshape(MLA_KV_RANK, MLA_HEADS, MLA_NOPE + MLA_V)
    out["wukn"] = wukv[:, :, :MLA_NOPE].reshape(MLA_KV_RANK, -1).astype(BF16)
    out["wuvt"] = wukv[:, :, MLA_NOPE:].reshape(MLA_KV_RANK, -1).T.astype(BF16)

    for name in ("gla_wo", "mla_wo", "conv_wo", "w_out", "ffn_wup", "ffn_wdown"):
        out[name] = p[name].astype(BF16)
    for name in ("conv_db", "conv_ln_g", "conv_ln_b", "ln1_g", "ln1_b", "ffn_db", "ln2_g", "ln2_b"):
        out[name] = p[name][None, :]
    out["conv_dw"], out["ffn_dw"] = p["conv_dw"], p["ffn_dw"]
    return out


def _stream(x, mods, w, tabs, gla_tables, states, ctx_keys, *, rope, last_ctx, alpha, tag):
    sh1, sc1, g1, sh2, sc2, g2 = mods
    t = x.shape[1]
    tm = min(t, 512)
    qkvr = _proj(x, sc1, sh1, [w["w_gla"]], [w["b_gla"]], act="none", out_dtype=BF16,
                 tn=1024, name=f"gla_proj_{tag}")
    gates = _proj(x, sc1, sh1, [w["w_gate"]], [w["b_gate"]], act="none", out_dtype=F32,
                  tn=HEAD_LANES, name=f"gate_proj_{tag}")
    pre_a, sf, sb = _gla(qkvr, gates, w, states[0], states[1], gla_tables, name=f"gla_{tag}")
    mla = _mla_proj(x, sc1, sh1, w, tabs, rope=rope, want_q=not last_ctx, tm=tm,
                    name=f"mla_proj_{tag}")
    keys = tuple(mla[-3:])
    if last_ctx:
        return None, (sf, sb), keys
    qn, qr = mla[0], mla[1]
    pre_b = _attn(qn, qr, [keys] + ([ctx_keys] if ctx_keys is not None else []), tq=tm,
                  name=f"attn_{tag}")
    glu = _proj(x, sc1, sh1, [w["w_glu_a"], w["w_glu_g"]], [w["b_glu_a"], w["b_glu_g"]],
                act="glu", out_dtype=BF16, tn=512, name=f"glu_proj_{tag}")
    pre_c = _conv(glu, w["conv_dw"], w["conv_db"], w["conv_ln_g"], w["conv_ln_b"],
                  name=f"conv_{tag}")
    mg = _proj(x, sc1, sh1, [w["w_mg"]], [w["b_mg"]], act="sigmoid", out_dtype=BF16,
               tn=1024, name=f"merge_gate_{tag}")
    x1 = _merge(pre_a, pre_b, pre_c, mg, x, g1, w, alpha=alpha, name=f"merge_{tag}")
    x2 = _ffn(x1, sc2, sh2, g2, w, alpha=alpha, name=f"ffn_{tag}")
    return x2, (sf, sb), keys


def kernel(x, c, ctx, c_ctx, w_ada, b_ada, w_in, b_in, gla_wa_f, gla_ba_f, gla_wa_b, gla_ba_b,
           gla_norm_g, gla_wo, mla_q_norm, mla_kv_norm, mla_wuq, mla_wukv, mla_wo, conv_dw,
           conv_db, conv_ln_g, conv_ln_b, conv_wo, w_out, ln1_g, ln1_b, ffn_wup, ffn_dw, ffn_db,
           ffn_wdown, ln2_g, ln2_b):
    stacked = dict(w_in=w_in, b_in=b_in, gla_wa_f=gla_wa_f, gla_ba_f=gla_ba_f, gla_wa_b=gla_wa_b,
                   gla_ba_b=gla_ba_b, gla_norm_g=gla_norm_g, gla_wo=gla_wo, mla_q_norm=mla_q_norm,
                   mla_kv_norm=mla_kv_norm, mla_wuq=mla_wuq, mla_wukv=mla_wukv, mla_wo=mla_wo,
                   conv_dw=conv_dw, conv_db=conv_db, conv_ln_g=conv_ln_g, conv_ln_b=conv_ln_b,
                   conv_wo=conv_wo, w_out=w_out, ln1_g=ln1_g, ln1_b=ln1_b, ffn_wup=ffn_wup,
                   ffn_dw=ffn_dw, ffn_db=ffn_db, ffn_wdown=ffn_wdown, ln2_g=ln2_g, ln2_b=ln2_b)
    depth = w_in.shape[0]
    bsz, t, d = x.shape
    alpha = float((2.0 * depth) ** 0.25)

    mod_rows = 16
    c_rows = jnp.zeros((mod_rows, d), F32).at[:bsz].set(c).at[bsz].set(c_ctx)
    mods = _ada(c_rows, w_ada, b_ada)

    tabs = _rope_tables(t)
    ctx_tabs = tuple(a[:ctx.shape[1]] for a in tabs)
    gla_tables = _gla_tables()
    zero_state = jnp.zeros((bsz, GLA_HEADS, GLA_DK, GLA_DV), F32)

    xc = ctx
    for l in range(depth):
        w = _prep_layer({k: v[l] for k, v in stacked.items()})
        lat_mods = [m[:, None, :] for m in jnp.split(mods[l, :bsz], 6, axis=-1)]
        ctx_mods = [m[:, None, :] for m in jnp.split(mods[l, bsz:bsz + 1], 6, axis=-1)]
        last = l == depth - 1
        xc, ctx_states, ctx_keys = _stream(xc, ctx_mods, w, ctx_tabs, gla_tables,
                                           (zero_state, zero_state), None, rope=False,
                                           last_ctx=last, alpha=alpha, tag="ctx")
        x, _, _ = _stream(x, lat_mods, w, tabs, gla_tables, ctx_states, ctx_keys, rope=True,
                          last_ctx=False, alpha=alpha, tag="lat")
    return x
```

```python
import functools

import numpy as np
import jax
import jax.numpy as jnp
from jax import lax
from jax.experimental import pallas as pl
from jax.experimental.pallas import tpu as pltpu

F32 = jnp.float32
BF16 = jnp.bfloat16

D_MODEL = 1024
GRID_W = 64

GLA_HEADS = 4
GLA_DK = 128
GLA_DV = 256
GLA_RANK = 16
GLA_TAU = 16.0
GLA_CHUNK = 128
GLA_LEVELS = 7
LOG2_E = 1.4426950408889634

MLA_HEADS = 8
MLA_Q_RANK = 384
MLA_KV_RANK = 256
MLA_NOPE = 128
MLA_ROPE = 64
MLA_V = 128
MLA_SCALE = (MLA_NOPE + MLA_ROPE) ** -0.5
MLA_QSCALE = MLA_SCALE * LOG2_E
MLA_ONES_ROWS = 16
MLA_VT_ROWS = MLA_V + MLA_ONES_ROWS
ROPE_F = MLA_ROPE // 4
ROPE_BASE = 10000.0
HEAD_LANES = 128

CONV_W = 31
CONV_HALO = 16
CONV_ROWS = 128
CONV_WINDOW = CONV_ROWS + 2 * CONV_HALO
CONV_SUB = 64
CONV_LANES = 256

D_FF = 2816
FFN_CHUNK = 256
FFN_HALO = 16

NORM_EPS = 1e-6

V7X_VMEM_LIMIT_BYTES = 56 * 1024 * 1024


def _params(semantics):
    return pltpu.CompilerParams(dimension_semantics=semantics,
                                vmem_limit_bytes=V7X_VMEM_LIMIT_BYTES)


def _dot(a, b):
    return jnp.dot(a, b, preferred_element_type=F32)


def _dot_nt(a, b):
    return lax.dot_general(a, b, (((1,), (1,)), ((), ())), preferred_element_type=F32)


def _dot_tn(a, b):
    return lax.dot_general(a, b, (((0,), (0,)), ((), ())), preferred_element_type=F32)


def _split2(x):
    hi = x.astype(BF16)
    lo = (x - hi.astype(F32)).astype(BF16)
    return hi, lo


def _layer_norm_rows(z, g, b):
    mu = jnp.mean(z, axis=-1, keepdims=True)
    zc = z - mu
    var = jnp.mean(zc * zc, axis=-1, keepdims=True)
    return zc * lax.rsqrt(var + NORM_EPS) * g + b


def _rms_rows(z, g):
    return z * lax.rsqrt(jnp.mean(z * z, axis=-1, keepdims=True) + NORM_EPS) * g


def _ada_body(c_ref, w_ref, b_ref, o_ref):
    c = c_ref[...]
    s = c * jax.nn.sigmoid(c)
    sh, sl = _split2(s)
    wh, wl = _split2(w_ref[...])
    o_ref[...] = _dot(sh, wh) + _dot(sh, wl) + _dot(sl, wh) + b_ref[...]


def _ada(c_rows, w_ada, b_ada):
    depth, k, n = w_ada.shape
    rows = c_rows.shape[0]
    tn = 512
    return pl.pallas_call(
        _ada_body,
        grid=(depth, n // tn),
        in_specs=[pl.BlockSpec((rows, k), lambda l, j: (0, 0)),
                  pl.BlockSpec((None, k, tn), lambda l, j: (l, 0, j)),
                  pl.BlockSpec((None, 1, tn), lambda l, j: (l, 0, j))],
        out_specs=pl.BlockSpec((None, rows, tn), lambda l, j: (l, 0, j)),
        out_shape=jax.ShapeDtypeStruct((depth, rows, n), F32),
        compiler_params=_params(("parallel", "parallel")),
        name="ada_mod",
    )(c_rows, w_ada, b_ada.reshape(depth, 1, n))


def _proj_body(x_ref, sc_ref, sh_ref, *rest, n_w, act):
    w_refs, b_refs = rest[:n_w], rest[n_w:2 * n_w]
    o_ref, h_ref = rest[2 * n_w], rest[2 * n_w + 1]

    @pl.when(pl.program_id(2) == 0)
    def _():
        x = x_ref[...].astype(F32)
        h_ref[...] = (x * (1.0 + sc_ref[...]) + sh_ref[...]).astype(BF16)

    h = h_ref[...]
    ys = [_dot(h, w[...]) + b[...] for w, b in zip(w_refs, b_refs)]
    if act == "glu":
        y = ys[0] * jax.nn.sigmoid(ys[1])
    elif act == "sigmoid":
        y = jax.nn.sigmoid(ys[0])
    else:
        y = ys[0]
    o_ref[...] = y.astype(o_ref.dtype)


def _mod_map(mod, n_grid):
    batched = mod.shape[0] > 1
    if n_grid == 3:
        return (lambda b, i, j: (b, 0, 0)) if batched else (lambda b, i, j: (0, 0, 0))
    return (lambda b, i: (b, 0, 0)) if batched else (lambda b, i: (0, 0, 0))


def _proj(x, scale, shift, ws, bs, *, act, out_dtype, tn, name):
    bsz, t, k = x.shape
    n = ws[0].shape[1]
    tm = min(t, 1024)
    mmap = _mod_map(scale, 3)
    in_specs = [pl.BlockSpec((None, tm, k), lambda b, i, j: (b, i, 0)),
                pl.BlockSpec((None, 1, k), mmap),
                pl.BlockSpec((None, 1, k), mmap)]
    in_specs += [pl.BlockSpec((k, tn), lambda b, i, j: (0, j)) for _ in ws]
    in_specs += [pl.BlockSpec((1, tn), lambda b, i, j: (0, j)) for _ in bs]
    return pl.pallas_call(
        functools.partial(_proj_body, n_w=len(ws), act=act),
        grid=(bsz, t // tm, n // tn),
        in_specs=in_specs,
        out_specs=pl.BlockSpec((None, tm, tn), lambda b, i, j: (b, i, j)),
        out_shape=jax.ShapeDtypeStruct((bsz, t, n), out_dtype),
        scratch_shapes=[pltpu.VMEM((tm, k), BF16)],
        compiler_params=_params(("parallel", "parallel", "arbitrary")),
        name=name,
    )(x, scale, shift, *ws, *bs)


def _rope(x, cs, sup, sdn):
    return x * cs + pltpu.roll(x, HEAD_LANES - ROPE_F, 1) * sup + pltpu.roll(x, ROPE_F, 1) * sdn


def _mla_proj_body(x_ref, sc_ref, sh_ref, wcq, bcq, wckv, bckv, wkr, bkr, gq, gkv,
                   wuqn, wuqr, wukn, wuvt, cs_ref, sup_ref, sdn_ref, *outs, rope, want_q):
    x = x_ref[...].astype(F32)
    h = (x * (1.0 + sc_ref[...]) + sh_ref[...]).astype(BF16)
    if want_q:
        qn_o, qr_o, kn_o, kr_o, vt_o = outs
    else:
        kn_o, kr_o, vt_o = outs

    ckv = _rms_rows(_dot(h, wckv[...]) + bckv[...], gkv[...]).astype(BF16)
    kn_o[...] = _dot(ckv, wukn[...]).astype(BF16)
    vt = _dot_nt(wuvt[...], ckv).astype(BF16)
    ones = jnp.ones((MLA_ONES_ROWS, vt.shape[1]), BF16)
    for hd in range(MLA_HEADS):
        vt_o[hd * MLA_VT_ROWS:hd * MLA_VT_ROWS + MLA_V, :] = vt[hd * MLA_V:(hd + 1) * MLA_V, :]
        vt_o[hd * MLA_VT_ROWS + MLA_V:(hd + 1) * MLA_VT_ROWS, :] = ones
    kr = _dot(h, wkr[...]) + bkr[...]
    if rope:
        kr = _rope(kr, cs_ref[...], sup_ref[...], sdn_ref[...])
    kr_o[...] = kr.astype(BF16)

    if want_q:
        cq = _rms_rows(_dot(h, wcq[...]) + bcq[...], gq[...]).astype(BF16)
        qn_o[...] = (_dot(cq, wuqn[...]) * MLA_QSCALE).astype(BF16)
        qr = _dot(cq, wuqr[...]) * MLA_QSCALE
        for hd in range(MLA_HEADS):
            sl = slice(hd * HEAD_LANES, (hd + 1) * HEAD_LANES)
            qh = qr[:, sl]
            if rope:
                qh = _rope(qh, cs_ref[...], sup_ref[...], sdn_ref[...])
            qr_o[:, sl] = qh.astype(BF16)


def _mla_proj(x, scale, shift, w, tabs, *, rope, want_q, tm, name):
    bsz, t, k = x.shape
    mmap = _mod_map(scale, 2)
    full = lambda a: pl.BlockSpec(a.shape, lambda b, i: (0,) * a.ndim)
    weights = [w["wcq"], w["bcq"], w["wckv"], w["bckv"], w["wkr"], w["bkr"], w["gq"], w["gkv"],
               w["wuqn"], w["wuqr"], w["wukn"], w["wuvt"]]
    hv = MLA_HEADS * HEAD_LANES
    row_spec = lambda width: pl.BlockSpec((None, tm, width), lambda b, i: (b, i, 0))
    out_specs, out_shape = [], []
    if want_q:
        out_specs += [row_spec(hv), row_spec(hv)]
        out_shape += [jax.ShapeDtypeStruct((bsz, t, hv), BF16)] * 2
    out_specs += [row_spec(hv), row_spec(HEAD_LANES),
                  pl.BlockSpec((None, None, MLA_HEADS * MLA_VT_ROWS, tm), lambda b, i: (b, i, 0, 0))]
    out_shape += [jax.ShapeDtypeStruct((bsz, t, hv), BF16),
                  jax.ShapeDtypeStruct((bsz, t, HEAD_LANES), BF16),
                  jax.ShapeDtypeStruct((bsz, t // tm, MLA_HEADS * MLA_VT_ROWS, tm), BF16)]
    tab_spec = pl.BlockSpec((tm, HEAD_LANES), lambda b, i: (i, 0))
    return pl.pallas_call(
        functools.partial(_mla_proj_body, rope=rope, want_q=want_q),
        grid=(bsz, t // tm),
        in_specs=[pl.BlockSpec((None, tm, k), lambda b, i: (b, i, 0)),
                  pl.BlockSpec((None, 1, k), mmap), pl.BlockSpec((None, 1, k), mmap)]
                 + [full(a) for a in weights] + [tab_spec] * 3,
        out_specs=out_specs,
        out_shape=out_shape,
        compiler_params=_params(("parallel", "parallel")),
        name=name,
    )(x, scale, shift, *weights, *tabs)


def _attn_body(qn_ref, qr_ref, *rest, chunks):
    n_src = len(chunks)
    o_ref = rest[3 * n_src]
    kcat_refs = rest[3 * n_src + 1:]

    @pl.when(pl.program_id(2) == 0)
    def _():
        for src in range(n_src):
            kcat_refs[src][:, 0:HEAD_LANES] = rest[3 * src][...]
            kcat_refs[src][:, HEAD_LANES:] = rest[3 * src + 1][...]

    q = jnp.concatenate([qn_ref[...], qr_ref[...]], axis=-1)
    tq = q.shape[0]
    steps = [(src, c, tk) for src, (n_chunks, tk) in enumerate(chunks) for c in range(n_chunks)]
    scores = lambda src, c, tk: _dot_nt(kcat_refs[src][c * tk:(c + 1) * tk, :], q)

    m = jnp.full((1, tq), -jnp.inf, F32)
    acc = jnp.zeros((MLA_VT_ROWS, tq), F32)
    s_next = scores(*steps[0])
    for idx, (src, c, tk) in enumerate(steps):
        s = s_next
        if idx + 1 < len(steps):
            s_next = scores(*steps[idx + 1])
        m_new = jnp.maximum(m, jnp.max(s, axis=0, keepdims=True))
        p = jnp.exp2(s - m_new)
        acc = jnp.exp2(m - m_new) * acc + _dot(rest[3 * src + 2][c], p.astype(BF16))
        m = m_new

    l = acc[MLA_V:MLA_V + 1, :]
    o_ref[...] = (acc[0:MLA_V, :] * (1.0 / l)).T.astype(o_ref.dtype)


def _attn(qn, qr, sources, *, tq, name):
    bsz, t, _ = qn.shape
    in_specs = [pl.BlockSpec((None, tq, HEAD_LANES), lambda b, h, i: (b, i, h))] * 2
    args, chunks, scratch = [qn, qr], [], []
    for kn, kr, vt in sources:
        tk_total, n_chunks, tk = kn.shape[1], vt.shape[1], vt.shape[3]
        in_specs += [pl.BlockSpec((None, tk_total, HEAD_LANES), lambda b, h, i: (b, 0, h)),
                     pl.BlockSpec((None, tk_total, HEAD_LANES), lambda b, h, i: (b, 0, 0)),
                     pl.BlockSpec((None, n_chunks, MLA_VT_ROWS, tk), lambda b, h, i: (b, 0, h, 0))]
        args += [kn, kr, vt]
        chunks.append((n_chunks, tk))
        scratch.append(pltpu.VMEM((tk_total, 2 * HEAD_LANES), BF16))
    return pl.pallas_call(
        functools.partial(_attn_body, chunks=tuple(chunks)),
        grid=(bsz, MLA_HEADS, t // tq),
        in_specs=in_specs,
        out_specs=pl.BlockSpec((None, tq, MLA_V), lambda b, h, i: (b, i, h)),
        out_shape=jax.ShapeDtypeStruct((bsz, t, MLA_HEADS * MLA_V), BF16),
        scratch_shapes=scratch,
        compiler_params=_params(("parallel", "parallel", "arbitrary")),
        name=name,
    )(*args)


def _gla_tables():
    c = GLA_CHUNK
    i = np.arange(c)
    r, m = i[:, None], i[None, :]
    sums = np.zeros((2, GLA_LEVELS + 2, c, c), bool)
    pair = np.zeros((2, GLA_LEVELS + 1, c, c), bool)
    role = np.zeros((2, GLA_LEVELS, c), bool)
    sums[0, 0], sums[1, 0] = m <= r, m >= r
    sums[0, -1], sums[1, -1] = m > r, m < r
    for lv in range(GLA_LEVELS):
        s = c >> lv
        mid = ((i // s) * s + s // 2)[:, None]
        late = ((i % s) >= s // 2)
        same = (i[:, None] // s) == (i[None, :] // s)
        sums[0, 1 + lv] = np.where(late[:, None], (m >= mid) & (m <= r), (m > r) & (m < mid))
        sums[1, 1 + lv] = np.where(late[:, None], (m >= mid) & (m < r), (m >= r) & (m < mid))
        role[0, lv], role[1, lv] = late, ~late
        pair[0, lv] = same & late[:, None] & ~late[None, :]
        pair[1, lv] = same & ~late[:, None] & late[None, :]
    pair[:, GLA_LEVELS] = np.eye(c, dtype=bool)
    sums = sums.reshape(2, (GLA_LEVELS + 2) * c, c)
    sums = np.concatenate([sums, sums], axis=2)
    role = np.broadcast_to(role[..., None], role.shape + (GLA_DK,))
    return (jnp.asarray(sums.astype(np.float32), dtype=BF16), jnp.asarray(pair.astype(np.float32)),
            jnp.asarray(role.astype(np.float32), dtype=BF16))


def _gla_decay_steps(g_ref, gate_w, sums_ref, ex_ref, tot_ref, d, slot, r0):
    c = GLA_CHUNK
    wh, wl, bias = gate_w
    gh, gl = _split2(g_ref[pl.ds(r0, c), :])
    z = _dot(gh, wh) + _dot(gh, wl) + _dot(gl, wh) + bias
    yield
    logd = (jnp.minimum(z, 0.0) - jnp.log1p(jnp.exp(-jnp.abs(z)))) * (LOG2_E / GLA_TAU)
    p1, p2 = _split2(logd)
    parts = jnp.concatenate([p1, p2], axis=0)
    for n in range(GLA_LEVELS + 2):
        span = slice(n * c, (n + 1) * c)
        ex = jnp.exp2(_dot(sums_ref[d, span, :], parts))
        ex_ref[d, slot, span, :] = ex.astype(BF16)
        if n == 0:
            total = ex[c - 1:c, :] if d == 0 else ex[0:1, :]
            tot_ref[d, slot] = jnp.broadcast_to(total, tot_ref.shape[2:])
        yield


def _gla_intra_steps(q_ref, k_ref, pair_ref, role_ref, ex_ref, d, slot, r0, box):
    c = GLA_CHUNK
    fwd = d == 0
    rows = pl.ds(r0, c)
    qb, kb = q_ref[rows, :], k_ref[rows, :]
    att = _dot_nt(qb, kb) * pair_ref[d, GLA_LEVELS]
    yield
    for lv in range(GLA_LEVELS):
        half = c >> (lv + 1)
        if half >= 16:
            x = jnp.concatenate([(qb if (b % 2 == 1) == fwd else kb)[b * half:(b + 1) * half, :]
                                 for b in range(c // half)], axis=0)
        else:
            x = jnp.where(role_ref[d, lv] > 0, qb, kb)
        x = x * ex_ref[d, slot, (1 + lv) * c:(2 + lv) * c, :]
        att = att + pair_ref[d, lv] * _dot_nt(x, x)
        yield
    box.append(att.astype(BF16))


def _gla_state_steps(q_ref, k_ref, v_ref, ex_ref, tot_ref, st_ref, d, items, out):
    c = GLA_CHUNK
    for slot, r0, box in items:
        rows = pl.ds(r0, c)
        block = lambda n: ex_ref[d, slot, n * c:(n + 1) * c, :]
        qb, kb, v = q_ref[rows, :], k_ref[rows, :], v_ref[rows, :]
        st = st_ref[...]
        out.append(_dot_nt(qb * block(0), st.astype(BF16)) + _dot(box[0], v))
        yield
        st_ref[...] = st * tot_ref[d, slot, 0:1, :] + _dot_tn(v, kb * block(GLA_LEVELS + 1))
        yield


def _interleave(chains):
    chains = list(chains)
    while chains:
        for ch in list(chains):
            if next(ch, StopIteration) is StopIteration:
                chains.remove(ch)


def _gla_body(q_ref, k_ref, v_ref, r_ref, g_ref, waf_ref, wab_ref, baf_ref, bab_ref, ng_ref,
              s0f_ref, s0b_ref, sums_ref, pair_ref, role_ref,
              y_ref, sf_ref, sb_ref,
              o_ref, stf_ref, stb_ref, ex_ref, tot_ref):
    t = q_ref.shape[0]
    c = GLA_CHUNK
    n = t // c

    stf_ref[...] = s0f_ref[...].T
    stb_ref[...] = s0b_ref[...].T
    gates = (_split2(waf_ref[...]) + (baf_ref[...],), _split2(wab_ref[...]) + (bab_ref[...],))
    states = (stf_ref, stb_ref)
    per = ex_ref.shape[1] // 2
    start = lambda d, p: pl.multiple_of((p if d == 0 else n - 1 - p) * c, c)
    decay = lambda d, p: _gla_decay_steps(g_ref, gates[d], sums_ref, ex_ref, tot_ref, d,
                                          p % (2 * per), start(d, jnp.minimum(p, n - 1)))

    _interleave(decay(d, p) for p in range(per) for d in range(2))

    def step(it):
        pos = [it * per + j for j in range(per)]
        boxes = {(d, j): [] for d in range(2) for j in range(per)}
        _interleave(
            [_gla_intra_steps(q_ref, k_ref, pair_ref, role_ref, ex_ref, d, pos[j] % (2 * per),
                              start(d, pos[j]), boxes[d, j]) for j in range(per) for d in range(2)]
            + [decay(d, p + per) for p in pos for d in range(2)])
        out = ([], [])
        _interleave(
            _gla_state_steps(q_ref, k_ref, v_ref, ex_ref, tot_ref, states[d], d,
                             [(pos[j] % (2 * per), start(d, pos[j]), boxes[d, j])
                              for j in range(per)], out[d]) for d in range(2))
        return [(start(d, pos[j]), out[d][j]) for d in range(2) for j in range(per)]

    def first_touch(it, carry):
        for r0, o in step(it):
            o_ref[pl.ds(r0, c), :] = o
        return carry

    def second_touch(it, carry):
        for r0, o in step(it):
            rows = pl.ds(r0, c)
            o = _rms_rows(o_ref[rows, :] + o, ng_ref[...])
            r = r_ref[rows, :].astype(F32)
            y_ref[rows, :] = (o * (r * jax.nn.sigmoid(r))).astype(y_ref.dtype)
        return carry

    lax.fori_loop(0, n // (2 * per), first_touch, 0)
    lax.fori_loop(n // (2 * per), n // per, second_touch, 0)

    sf_ref[...] = stf_ref[...].T
    sb_ref[...] = stb_ref[...].T


def _gla(qkvr, gates, w, s0f, s0b, tables, *, name):
    bsz, t, _ = qkvr.shape
    assert t % (2 * GLA_CHUNK) == 0
    per = 1
    sums, pair, role = tables
    kv0 = GLA_HEADS * GLA_DK // GLA_DK
    v0 = 2 * GLA_HEADS * GLA_DK // GLA_DV
    r0 = v0 + GLA_HEADS
    col = lambda width, off: pl.BlockSpec((None, t, width), lambda b, h: (b, 0, off + h))
    head = lambda a: pl.BlockSpec((None,) + a.shape[1:], lambda b, h: (h,) + (0,) * (a.ndim - 1))
    full = lambda a: pl.BlockSpec(a.shape, lambda b, h: (0,) * a.ndim)
    state = pl.BlockSpec((None, None, GLA_DK, GLA_DV), lambda b, h: (b, h, 0, 0))
    return pl.pallas_call(
        _gla_body,
        grid=(bsz, GLA_HEADS),
        in_specs=[col(GLA_DK, 0), col(GLA_DK, kv0), col(GLA_DV, v0), col(GLA_DV, r0),
                  pl.BlockSpec((None, t, HEAD_LANES), lambda b, h: (b, 0, 0)),
                  head(w["waf"]), head(w["wab"]), head(w["baf"]), head(w["bab"]), head(w["ng"]),
                  state, state, full(sums), full(pair), full(role)],
        out_specs=[pl.BlockSpec((None, t, GLA_DV), lambda b, h: (b, 0, h)), state, state],
        out_shape=[jax.ShapeDtypeStruct((bsz, t, GLA_HEADS * GLA_DV), BF16),
                   jax.ShapeDtypeStruct((bsz, GLA_HEADS, GLA_DK, GLA_DV), F32),
                   jax.ShapeDtypeStruct((bsz, GLA_HEADS, GLA_DK, GLA_DV), F32)],
        scratch_shapes=[pltpu.VMEM((t, GLA_DV), F32),
                        pltpu.VMEM((GLA_DV, GLA_DK), F32), pltpu.VMEM((GLA_DV, GLA_DK), F32),
                        pltpu.VMEM((2, 2 * per, (GLA_LEVELS + 2) * GLA_CHUNK, GLA_DK), BF16),
                        pltpu.VMEM((2, 2 * per, 8, GLA_DK), F32)],
        compiler_params=_params(("parallel", "parallel")),
        name=name,
    )(qkvr, qkvr, qkvr, qkvr, gates, w["waf"], w["wab"], w["baf"], w["bab"], w["ng"],
      s0f, s0b, sums, pair, role)


def _conv_shift_table():
    r = np.arange(CONV_WINDOW)
    return jnp.asarray(np.stack([(r[None, :] == r[:, None] + ph) for ph in range(8)])
                       .astype(np.float32), dtype=BF16)


def _conv_body(xm_ref, xp_ref, xn_ref, shift_ref, dw_ref, db_ref, lg_ref, lb_ref, o_ref,
               buf_ref, win_ref, y_ref):
    tt = xm_ref.shape[0]
    i = pl.program_id(1)
    keep_p = jnp.where(i == 0, 0.0, 1.0).astype(BF16)
    keep_n = jnp.where(i == pl.num_programs(1) - 1, 0.0, 1.0).astype(BF16)
    buf_ref[0:CONV_HALO, :] = xp_ref[...] * keep_p
    buf_ref[CONV_HALO:CONV_HALO + tt, :] = xm_ref[...]
    buf_ref[CONV_HALO + tt:, :] = xn_ref[...] * keep_n

    def row_block(rb, carry):
        r0 = pl.multiple_of(rb * CONV_ROWS, CONV_ROWS)
        window = buf_ref[pl.ds(r0, CONV_WINDOW), :]
        for ph in range(8):
            win_ref[ph] = _dot(shift_ref[ph], window)
        for rs in range(CONV_ROWS // CONV_SUB):
            for cc in range(D_MODEL // CONV_LANES):
                lanes = slice(cc * CONV_LANES, (cc + 1) * CONV_LANES)
                acc = jnp.zeros((CONV_SUB, CONV_LANES), F32) + db_ref[:, lanes]
                for tap in range(CONV_W):
                    a, ph = divmod(tap + 1, 8)
                    lo = rs * CONV_SUB + 8 * a
                    acc = acc + dw_ref[tap:tap + 1, lanes] * win_ref[ph, lo:lo + CONV_SUB, lanes]
                y_ref[pl.ds(r0 + rs * CONV_SUB, CONV_SUB), lanes] = acc
        return carry

    lax.fori_loop(0, tt // CONV_ROWS, row_block, 0)
    z = _layer_norm_rows(y_ref[...], lg_ref[...], lb_ref[...])
    o_ref[...] = (z * jax.nn.sigmoid(z)).astype(o_ref.dtype)


def _conv(glu, dw, db, lg, lb, *, name):
    bsz, t, ch = glu.shape
    tt = min(t, 512)
    nh = tt // CONV_HALO
    n_halo = t // CONV_HALO
    shift = _conv_shift_table()
    full = lambda a: pl.BlockSpec(a.shape, lambda b, i: (0,) * a.ndim)
    return pl.pallas_call(
        _conv_body,
        grid=(bsz, t // tt),
        in_specs=[pl.BlockSpec((None, tt, ch), lambda b, i: (b, i, 0)),
                  pl.BlockSpec((None, CONV_HALO, ch),
                               lambda b, i: (b, jnp.maximum(i * nh - 1, 0), 0)),
                  pl.BlockSpec((None, CONV_HALO, ch),
                               lambda b, i: (b, jnp.minimum((i + 1) * nh, n_halo - 1), 0)),
                  full(shift), full(dw), full(db), full(lg), full(lb)],
        out_specs=pl.BlockSpec((None, tt, ch), lambda b, i: (b, i, 0)),
        out_shape=jax.ShapeDtypeStruct((bsz, t, ch), BF16),
        scratch_shapes=[pltpu.VMEM((tt + 2 * CONV_HALO, ch), BF16),
                        pltpu.VMEM((8, CONV_WINDOW, ch), F32), pltpu.VMEM((tt, ch), F32)],
        compiler_params=_params(("parallel", "parallel")),
        name=name,
    )(glu, glu, glu, shift, dw, db, lg, lb)


def _merge_body(pa_ref, pb_ref, pc_ref, gt_ref, x_ref, g1_ref, wa, wb, wc, wo, lg_ref, lb_ref,
                o_ref, *, alpha):
    d = D_MODEL
    m = gt_ref[:, 0:d].astype(F32) * _dot(pa_ref[...], wa[...])
    m = m + gt_ref[:, d:2 * d].astype(F32) * _dot(pb_ref[...], wb[...])
    m = m + gt_ref[:, 2 * d:3 * d].astype(F32) * _dot(pc_ref[...], wc[...])
    y = _dot(m.astype(BF16), wo[...])
    z = alpha * x_ref[...] + g1_ref[...] * y
    o_ref[...] = _layer_norm_rows(z, lg_ref[...], lb_ref[...])


def _merge(pa, pb, pc, gates, x, g1, w, *, alpha, name):
    bsz, t, d = x.shape
    tm = min(t, 512)
    row = lambda width: pl.BlockSpec((None, tm, width), lambda b, i: (b, i, 0))
    full = lambda a: pl.BlockSpec(a.shape, lambda b, i: (0,) * a.ndim,
                                  pipeline_mode=pl.Buffered(1))
    weights = [w["gla_wo"], w["mla_wo"], w["conv_wo"], w["w_out"], w["ln1_g"], w["ln1_b"]]
    return pl.pallas_call(
        functools.partial(_merge_body, alpha=alpha),
        grid=(bsz, t // tm),
        in_specs=[row(d), row(d), row(d), row(3 * d), row(d),
                  pl.BlockSpec((None, 1, d), _mod_map(g1, 2))] + [full(a) for a in weights],
        out_specs=row(d),
        out_shape=jax.ShapeDtypeStruct((bsz, t, d), F32),
        compiler_params=_params(("parallel", "parallel")),
        name=name,
    )(pa, pb, pc, gates, x, g1, *weights)


def _ffn_body(xm_ref, xp_ref, xn_ref, sc_ref, sh_ref, g2_ref, wup_ref, dw_ref, db_ref, wd_ref,
              lg_ref, lb_ref, o_ref, *, alpha):
    tm = xm_ref.shape[0]
    i = pl.program_id(1)
    rows = tm + 2 * FFN_HALO
    mid = slice(FFN_HALO, FFN_HALO + tm)

    mod = lambda x: x * (1.0 + sc_ref[...]) + sh_ref[...]
    keep_p = jnp.where(i == 0, 0.0, 1.0)
    keep_n = jnp.where(i == pl.num_programs(1) - 1, 0.0, 1.0)
    h = jnp.concatenate([(mod(xp_ref[...]) * keep_p).astype(BF16),
                         mod(xm_ref[...]).astype(BF16),
                         (mod(xn_ref[...]) * keep_n).astype(BF16)], axis=0)

    n_chunks = D_FF // FFN_CHUNK
    cols = lambda c: (slice(c * FFN_CHUNK, (c + 1) * FFN_CHUNK),
                      slice(D_FF + c * FFN_CHUNK, D_FF + (c + 1) * FFN_CHUNK))
    up = lambda c: tuple(_dot(h, wup_ref[:, col]) for col in cols(c))

    def conv3(u, col):
        prev = pltpu.roll(u, 1, 0)[mid, :]
        nxt = pltpu.roll(u, rows - 1, 0)[mid, :]
        return (dw_ref[0:1, col] * prev + dw_ref[1:2, col] * u[mid, :] + dw_ref[2:3, col] * nxt
                + db_ref[:, col])

    y = None
    u_next = up(0)
    for c in range(n_chunks):
        ug, uv = u_next
        if c + 1 < n_chunks:
            u_next = up(c + 1)
        cg, cv = conv3(ug, cols(c)[0]), conv3(uv, cols(c)[1])
        a = (cg * jax.nn.sigmoid(cg) * cv).astype(BF16)
        part = _dot(a, wd_ref[c * FFN_CHUNK:(c + 1) * FFN_CHUNK, :])
        y = part if y is None else y + part

    z = alpha * xm_ref[...] + g2_ref[...] * y
    o_ref[...] = _layer_norm_rows(z, lg_ref[...], lb_ref[...])


def _ffn(x, scale, shift, g2, w, *, alpha, name):
    bsz, t, d = x.shape
    tm = min(t, 1024)
    nh = tm // FFN_HALO
    n_halo = t // FFN_HALO
    mmap = _mod_map(scale, 2)
    resident = lambda a: pl.BlockSpec(a.shape, lambda b, i: (0,) * a.ndim,
                                      pipeline_mode=pl.Buffered(1))
    weights = [w["ffn_wup"], w["ffn_dw"], w["ffn_db"], w["ffn_wdown"], w["ln2_g"], w["ln2_b"]]
    return pl.pallas_call(
        functools.partial(_ffn_body, alpha=alpha),
        grid=(bsz, t // tm),
        in_specs=[pl.BlockSpec((None, tm, d), lambda b, i: (b, i, 0)),
                  pl.BlockSpec((None, FFN_HALO, d),
                               lambda b, i: (b, jnp.maximum(i * nh - 1, 0), 0)),
                  pl.BlockSpec((None, FFN_HALO, d),
                               lambda b, i: (b, jnp.minimum((i + 1) * nh, n_halo - 1), 0)),
                  pl.BlockSpec((None, 1, d), mmap), pl.BlockSpec((None, 1, d), mmap),
                  pl.BlockSpec((None, 1, d), mmap)] + [resident(a) for a in weights],
        out_specs=pl.BlockSpec((None, tm, d), lambda b, i: (b, i, 0)),
        out_shape=jax.ShapeDtypeStruct((bsz, t, d), F32),
        compiler_params=_params(("parallel", "parallel")),
        name=name,
    )(x, x, x, scale, shift, g2, *weights)


def _rope_tables(t):
    rows = t // GRID_W
    row = jnp.repeat(jnp.arange(rows, dtype=F32), GRID_W)
    colv = jnp.tile(jnp.arange(GRID_W, dtype=F32), rows)
    inv = ROPE_BASE ** (-2.0 * jnp.arange(ROPE_F, dtype=F32) / (MLA_ROPE // 2))
    ang = jnp.concatenate([row[:, None] * inv, row[:, None] * inv,
                           colv[:, None] * inv, colv[:, None] * inv], axis=1)
    cos, sin = jnp.cos(ang), jnp.sin(ang)
    lane = jnp.arange(MLA_ROPE)
    first = (lane % (2 * ROPE_F)) < ROPE_F
    pad = lambda a: jnp.pad(a, ((0, 0), (0, HEAD_LANES - MLA_ROPE)))
    return (pad(cos), pad(jnp.where(first, -sin, 0.0)), pad(jnp.where(first, 0.0, sin)))


def _prep_layer(p):
    d = D_MODEL
    w_in, b_in = p["w_in"], p["b_in"]
    edges = np.cumsum([0, 512, 512, 1024, 1024, 16, 16, MLA_Q_RANK, MLA_KV_RANK, MLA_ROPE, 2 * d, 3 * d])
    col = lambda a, b: (w_in[:, edges[a]:edges[b]].astype(BF16), b_in[edges[a]:edges[b]][None, :])
    out = {}
    qscale = jnp.where(jnp.arange(edges[4]) < edges[1], GLA_DK ** -0.5, 1.0)
    out["w_gla"] = (w_in[:, :edges[4]] * qscale).astype(BF16)
    out["b_gla"] = (b_in[:edges[4]] * qscale)[None, :]
    wg, bg = col(4, 6)
    out["w_gate"] = jnp.pad(wg, ((0, 0), (0, HEAD_LANES - 2 * GLA_RANK)))
    out["b_gate"] = jnp.pad(bg, ((0, 0), (0, HEAD_LANES - 2 * GLA_RANK)))
    out["wcq"], out["bcq"] = col(6, 7)
    out["wckv"], out["bckv"] = col(7, 8)
    wkr, bkr = col(8, 9)
    out["wkr"] = jnp.pad(wkr, ((0, 0), (0, HEAD_LANES - MLA_ROPE)))
    out["bkr"] = jnp.pad(bkr, ((0, 0), (0, HEAD_LANES - MLA_ROPE)))
    wconv, bconv = col(9, 10)
    out["w_glu_a"], out["w_glu_g"] = wconv[:, :d], wconv[:, d:]
    out["b_glu_a"], out["b_glu_g"] = bconv[:, :d], bconv[:, d:]
    out["w_mg"], out["b_mg"] = col(10, 11)

    def gate_w(wa, lo):
        wa = wa.reshape(GLA_RANK, GLA_HEADS, GLA_DK).transpose(1, 0, 2)
        return jnp.pad(wa, ((0, 0), (lo, HEAD_LANES - GLA_RANK - lo), (0, 0)))
    out["waf"] = gate_w(p["gla_wa_f"], 0)
    out["wab"] = gate_w(p["gla_wa_b"], GLA_RANK)
    out["baf"] = p["gla_ba_f"].reshape(GLA_HEADS, 1, GLA_DK)
    out["bab"] = p["gla_ba_b"].reshape(GLA_HEADS, 1, GLA_DK)
    out["ng"] = p["gla_norm_g"].reshape(GLA_HEADS, 1, GLA_DV)

    out["gq"] = p["mla_q_norm"][None, :]
    out["gkv"] = p["mla_kv_norm"][None, :]
    wuq = p["mla_wuq"].reshape(MLA_Q_RANK, MLA_HEADS, MLA_NOPE + MLA_ROPE)
    out["wuqn"] = wuq[:, :, :MLA_NOPE].reshape(MLA_Q_RANK, -1).astype(BF16)
    out["wuqr"] = jnp.pad(wuq[:, :, MLA_NOPE:], ((0, 0), (0, 0), (0, HEAD_LANES - MLA_ROPE))
                          ).reshape(MLA_Q_RANK, -1).astype(BF16)
    wukv = p["mla_wukv"].reshape(MLA_KV_RANK, MLA_HEADS, MLA_NOPE + MLA_V)
    out["wukn"] = wukv[:, :, :MLA_NOPE].reshape(MLA_KV_RANK, -1).astype(BF16)
    out["wuvt"] = wukv[:, :, MLA_NOPE:].reshape(MLA_KV_RANK, -1).T.astype(BF16)

    for name in ("gla_wo", "mla_wo", "conv_wo", "w_out", "ffn_wup", "ffn_wdown"):
        out[name] = p[name].astype(BF16)
    for name in ("conv_db", "conv_ln_g", "conv_ln_b", "ln1_g", "ln1_b", "ffn_db", "ln2_g", "ln2_b"):
        out[name] = p[name][None, :]
    out["conv_dw"], out["ffn_dw"] = p["conv_dw"], p["ffn_dw"]
    return out


def _stream(x, mods, w, tabs, gla_tables, states, ctx_keys, *, rope, last_ctx, alpha, tag):
    sh1, sc1, g1, sh2, sc2, g2 = mods
    t = x.shape[1]
    tm = min(t, 512)
    qkvr = _proj(x, sc1, sh1, [w["w_gla"]], [w["b_gla"]], act="none", out_dtype=BF16,
                 tn=1024, name=f"gla_proj_{tag}")
    gates = _proj(x, sc1, sh1, [w["w_gate"]], [w["b_gate"]], act="none", out_dtype=F32,
                  tn=HEAD_LANES, name=f"gate_proj_{tag}")
    pre_a, sf, sb = _gla(qkvr, gates, w, states[0], states[1], gla_tables, name=f"gla_{tag}")
    mla = _mla_proj(x, sc1, sh1, w, tabs, rope=rope, want_q=not last_ctx, tm=tm,
                    name=f"mla_proj_{tag}")
    keys = tuple(mla[-3:])
    if last_ctx:
        return None, (sf, sb), keys
    qn, qr = mla[0], mla[1]
    pre_b = _attn(qn, qr, [keys] + ([ctx_keys] if ctx_keys is not None else []), tq=min(t, 2048),
                  name=f"attn_{tag}")
    glu = _proj(x, sc1, sh1, [w["w_glu_a"], w["w_glu_g"]], [w["b_glu_a"], w["b_glu_g"]],
                act="glu", out_dtype=BF16, tn=512, name=f"glu_proj_{tag}")
    pre_c = _conv(glu, w["conv_dw"], w["conv_db"], w["conv_ln_g"], w["conv_ln_b"],
                  name=f"conv_{tag}")
    mg = _proj(x, sc1, sh1, [w["w_mg"]], [w["b_mg"]], act="sigmoid", out_dtype=BF16,
               tn=1024, name=f"merge_gate_{tag}")
    x1 = _merge(pre_a, pre_b, pre_c, mg, x, g1, w, alpha=alpha, name=f"merge_{tag}")
    x2 = _ffn(x1, sc2, sh2, g2, w, alpha=alpha, name=f"ffn_{tag}")
    return x2, (sf, sb), keys


def kernel(x, c, ctx, c_ctx, w_ada, b_ada, w_in, b_in, gla_wa_f, gla_ba_f, gla_wa_b, gla_ba_b,
           gla_norm_g, gla_wo, mla_q_norm, mla_kv_norm, mla_wuq, mla_wukv, mla_wo, conv_dw,
           conv_db, conv_ln_g, conv_ln_b, conv_wo, w_out, ln1_g, ln1_b, ffn_wup, ffn_dw, ffn_db,
           ffn_wdown, ln2_g, ln2_b):
    stacked = dict(w_in=w_in, b_in=b_in, gla_wa_f=gla_wa_f, gla_ba_f=gla_ba_f, gla_wa_b=gla_wa_b,
                   gla_ba_b=gla_ba_b, gla_norm_g=gla_norm_g, gla_wo=gla_wo, mla_q_norm=mla_q_norm,
                   mla_kv_norm=mla_kv_norm, mla_wuq=mla_wuq, mla_wukv=mla_wukv, mla_wo=mla_wo,
                   conv_dw=conv_dw, conv_db=conv_db, conv_ln_g=conv_ln_g, conv_ln_b=conv_ln_b,
                   conv_wo=conv_wo, w_out=w_out, ln1_g=ln1_g, ln1_b=ln1_b, ffn_wup=ffn_wup,
                   ffn_dw=ffn_dw, ffn_db=ffn_db, ffn_wdown=ffn_wdown, ln2_g=ln2_g, ln2_b=ln2_b)
    depth = w_in.shape[0]
    bsz, t, d = x.shape
    alpha = float((2.0 * depth) ** 0.25)

    mod_rows = 16
    c_rows = jnp.zeros((mod_rows, d), F32).at[:bsz].set(c).at[bsz].set(c_ctx)
    mods = _ada(c_rows, w_ada, b_ada)

    tabs = _rope_tables(t)
    ctx_tabs = tuple(a[:ctx.shape[1]] for a in tabs)
    gla_tables = _gla_tables()
    zero_state = jnp.zeros((bsz, GLA_HEADS, GLA_DK, GLA_DV), F32)

    xc = ctx
    for l in range(depth):
        w = _prep_layer({k: v[l] for k, v in stacked.items()})
        lat_mods = [m[:, None, :] for m in jnp.split(mods[l, :bsz], 6, axis=-1)]
        ctx_mods = [m[:, None, :] for m in jnp.split(mods[l, bsz:bsz + 1], 6, axis=-1)]
        last = l == depth - 1
        xc, ctx_states, ctx_keys = _stream(xc, ctx_mods, w, ctx_tabs, gla_tables,
                                           (zero_state, zero_state), None, rope=False,
                                           last_ctx=last, alpha=alpha, tag="ctx")
        x, _, _ = _stream(x, lat_mods, w, tabs, gla_tables, ctx_states, ctx_keys, rope=True,
                          last_ctx=False, alpha=alpha, tag="lat")
    return x
```

```python
import functools

import numpy as np
import jax
import jax.numpy as jnp
from jax import lax
from jax.experimental import pallas as pl
from jax.experimental.pallas import tpu as pltpu

F32 = jnp.float32
BF16 = jnp.bfloat16

D_MODEL = 1024
GRID_W = 64

GLA_HEADS = 4
GLA_DK = 128
GLA_DV = 256
GLA_RANK = 16
GLA_TAU = 16.0
GLA_CHUNK = 128
GLA_LEVELS = 7
GLA_VPU_LEVELS = 1
LOG2_E = 1.4426950408889634

MLA_HEADS = 8
MLA_Q_RANK = 384
MLA_KV_RANK = 256
MLA_NOPE = 128
MLA_ROPE = 64
MLA_V = 128
MLA_SCALE = (MLA_NOPE + MLA_ROPE) ** -0.5
MLA_QSCALE = MLA_SCALE * LOG2_E
MLA_ONES_ROWS = 16
MLA_VT_ROWS = MLA_V + MLA_ONES_ROWS
ROPE_F = MLA_ROPE // 4
ROPE_BASE = 10000.0
HEAD_LANES = 128

CONV_W = 31
CONV_HALO = 16
CONV_ROWS = 128
CONV_WINDOW = CONV_ROWS + 2 * CONV_HALO
CONV_SUB = 64
CONV_LANES = 256

D_FF = 2816
FFN_CHUNK = 256
FFN_HALO = 16

NORM_EPS = 1e-6

V7X_VMEM_LIMIT_BYTES = 56 * 1024 * 1024


def _params(semantics):
    return pltpu.CompilerParams(dimension_semantics=semantics,
                                vmem_limit_bytes=V7X_VMEM_LIMIT_BYTES)


def _dot(a, b):
    return jnp.dot(a, b, preferred_element_type=F32)


def _dot_nt(a, b):
    return lax.dot_general(a, b, (((1,), (1,)), ((), ())), preferred_element_type=F32)


def _dot_tn(a, b):
    return lax.dot_general(a, b, (((0,), (0,)), ((), ())), preferred_element_type=F32)


def _split2(x):
    hi = x.astype(BF16)
    lo = (x - hi.astype(F32)).astype(BF16)
    return hi, lo


def _layer_norm_rows(z, g, b):
    mu = jnp.mean(z, axis=-1, keepdims=True)
    zc = z - mu
    var = jnp.mean(zc * zc, axis=-1, keepdims=True)
    return zc * lax.rsqrt(var + NORM_EPS) * g + b


def _rms_rows(z, g):
    return z * lax.rsqrt(jnp.mean(z * z, axis=-1, keepdims=True) + NORM_EPS) * g


def _ada_body(c_ref, w_ref, b_ref, o_ref):
    c = c_ref[...]
    s = c * jax.nn.sigmoid(c)
    sh, sl = _split2(s)
    wh, wl = _split2(w_ref[...])
    o_ref[...] = _dot(sh, wh) + _dot(sh, wl) + _dot(sl, wh) + b_ref[...]


def _ada(c_rows, w_ada, b_ada):
    depth, k, n = w_ada.shape
    rows = c_rows.shape[0]
    tn = 512
    return pl.pallas_call(
        _ada_body,
        grid=(depth, n // tn),
        in_specs=[pl.BlockSpec((rows, k), lambda l, j: (0, 0)),
                  pl.BlockSpec((None, k, tn), lambda l, j: (l, 0, j)),
                  pl.BlockSpec((None, 1, tn), lambda l, j: (l, 0, j))],
        out_specs=pl.BlockSpec((None, rows, tn), lambda l, j: (l, 0, j)),
        out_shape=jax.ShapeDtypeStruct((depth, rows, n), F32),
        compiler_params=_params(("parallel", "parallel")),
        name="ada_mod",
    )(c_rows, w_ada, b_ada.reshape(depth, 1, n))


def _proj_body(x_ref, sc_ref, sh_ref, *rest, n_w, act):
    w_refs, b_refs = rest[:n_w], rest[n_w:2 * n_w]
    o_ref, h_ref = rest[2 * n_w], rest[2 * n_w + 1]

    @pl.when(pl.program_id(2) == 0)
    def _():
        x = x_ref[...].astype(F32)
        h_ref[...] = (x * (1.0 + sc_ref[...]) + sh_ref[...]).astype(BF16)

    h = h_ref[...]
    ys = [_dot(h, w[...]) + b[...] for w, b in zip(w_refs, b_refs)]
    if act == "glu":
        y = ys[0] * jax.nn.sigmoid(ys[1])
    elif act == "sigmoid":
        y = jax.nn.sigmoid(ys[0])
    else:
        y = ys[0]
    o_ref[...] = y.astype(o_ref.dtype)


def _mod_map(mod, n_grid):
    batched = mod.shape[0] > 1
    if n_grid == 3:
        return (lambda b, i, j: (b, 0, 0)) if batched else (lambda b, i, j: (0, 0, 0))
    return (lambda b, i: (b, 0, 0)) if batched else (lambda b, i: (0, 0, 0))


def _proj(x, scale, shift, ws, bs, *, act, out_dtype, tn, name):
    bsz, t, k = x.shape
    n = ws[0].shape[1]
    tm = min(t, 1024)
    mmap = _mod_map(scale, 3)
    in_specs = [pl.BlockSpec((None, tm, k), lambda b, i, j: (b, i, 0)),
                pl.BlockSpec((None, 1, k), mmap),
                pl.BlockSpec((None, 1, k), mmap)]
    in_specs += [pl.BlockSpec((k, tn), lambda b, i, j: (0, j)) for _ in ws]
    in_specs += [pl.BlockSpec((1, tn), lambda b, i, j: (0, j)) for _ in bs]
    return pl.pallas_call(
        functools.partial(_proj_body, n_w=len(ws), act=act),
        grid=(bsz, t // tm, n // tn),
        in_specs=in_specs,
        out_specs=pl.BlockSpec((None, tm, tn), lambda b, i, j: (b, i, j)),
        out_shape=jax.ShapeDtypeStruct((bsz, t, n), out_dtype),
        scratch_shapes=[pltpu.VMEM((tm, k), BF16)],
        compiler_params=_params(("parallel", "parallel", "arbitrary")),
        name=name,
    )(x, scale, shift, *ws, *bs)


def _rope(x, cs, sup, sdn):
    return x * cs + pltpu.roll(x, HEAD_LANES - ROPE_F, 1) * sup + pltpu.roll(x, ROPE_F, 1) * sdn


def _mla_proj_body(x_ref, sc_ref, sh_ref, wcq, bcq, wckv, bckv, wkr, bkr, gq, gkv,
                   wuqn, wuqr, wukn, wuvt, cs_ref, sup_ref, sdn_ref, *outs, rope, want_q):
    x = x_ref[...].astype(F32)
    h = (x * (1.0 + sc_ref[...]) + sh_ref[...]).astype(BF16)
    if want_q:
        qn_o, qr_o, kn_o, kr_o, vt_o = outs
    else:
        kn_o, kr_o, vt_o = outs

    ckv = _rms_rows(_dot(h, wckv[...]) + bckv[...], gkv[...]).astype(BF16)
    kn_o[...] = _dot(ckv, wukn[...]).astype(BF16)
    vt = _dot_nt(wuvt[...], ckv).astype(BF16)
    ones = jnp.ones((MLA_ONES_ROWS, vt.shape[1]), BF16)
    for hd in range(MLA_HEADS):
        vt_o[hd * MLA_VT_ROWS:hd * MLA_VT_ROWS + MLA_V, :] = vt[hd * MLA_V:(hd + 1) * MLA_V, :]
        vt_o[hd * MLA_VT_ROWS + MLA_V:(hd + 1) * MLA_VT_ROWS, :] = ones
    kr = _dot(h, wkr[...]) + bkr[...]
    if rope:
        kr = _rope(kr, cs_ref[...], sup_ref[...], sdn_ref[...])
    kr_o[...] = kr.astype(BF16)

    if want_q:
        cq = _rms_rows(_dot(h, wcq[...]) + bcq[...], gq[...]).astype(BF16)
        qn_o[...] = (_dot(cq, wuqn[...]) * MLA_QSCALE).astype(BF16)
        qr = _dot(cq, wuqr[...]) * MLA_QSCALE
        for hd in range(MLA_HEADS):
            sl = slice(hd * HEAD_LANES, (hd + 1) * HEAD_LANES)
            qh = qr[:, sl]
            if rope:
                qh = _rope(qh, cs_ref[...], sup_ref[...], sdn_ref[...])
            qr_o[:, sl] = qh.astype(BF16)


def _mla_proj(x, scale, shift, w, tabs, *, rope, want_q, tm, name):
    bsz, t, k = x.shape
    mmap = _mod_map(scale, 2)
    full = lambda a: pl.BlockSpec(a.shape, lambda b, i: (0,) * a.ndim)
    weights = [w["wcq"], w["bcq"], w["wckv"], w["bckv"], w["wkr"], w["bkr"], w["gq"], w["gkv"],
               w["wuqn"], w["wuqr"], w["wukn"], w["wuvt"]]
    hv = MLA_HEADS * HEAD_LANES
    row_spec = lambda width: pl.BlockSpec((None, tm, width), lambda b, i: (b, i, 0))
    out_specs, out_shape = [], []
    if want_q:
        out_specs += [row_spec(hv), row_spec(hv)]
        out_shape += [jax.ShapeDtypeStruct((bsz, t, hv), BF16)] * 2
    out_specs += [row_spec(hv), row_spec(HEAD_LANES),
                  pl.BlockSpec((None, None, MLA_HEADS * MLA_VT_ROWS, tm), lambda b, i: (b, i, 0, 0))]
    out_shape += [jax.ShapeDtypeStruct((bsz, t, hv), BF16),
                  jax.ShapeDtypeStruct((bsz, t, HEAD_LANES), BF16),
                  jax.ShapeDtypeStruct((bsz, t // tm, MLA_HEADS * MLA_VT_ROWS, tm), BF16)]
    tab_spec = pl.BlockSpec((tm, HEAD_LANES), lambda b, i: (i, 0))
    return pl.pallas_call(
        functools.partial(_mla_proj_body, rope=rope, want_q=want_q),
        grid=(bsz, t // tm),
        in_specs=[pl.BlockSpec((None, tm, k), lambda b, i: (b, i, 0)),
                  pl.BlockSpec((None, 1, k), mmap), pl.BlockSpec((None, 1, k), mmap)]
                 + [full(a) for a in weights] + [tab_spec] * 3,
        out_specs=out_specs,
        out_shape=out_shape,
        compiler_params=_params(("parallel", "parallel")),
        name=name,
    )(x, scale, shift, *weights, *tabs)


def _attn_body(qn_ref, qr_ref, *rest, chunks):
    n_src = len(chunks)
    o_ref = rest[3 * n_src]
    kcat_refs = rest[3 * n_src + 1:]

    @pl.when(pl.program_id(2) == 0)
    def _():
        for src in range(n_src):
            kcat_refs[src][:, 0:HEAD_LANES] = rest[3 * src][...]
            kcat_refs[src][:, HEAD_LANES:] = rest[3 * src + 1][...]

    q = jnp.concatenate([qn_ref[...], qr_ref[...]], axis=-1)
    tq = q.shape[0]
    steps = [(src, c, tk) for src, (n_chunks, tk) in enumerate(chunks) for c in range(n_chunks)]
    scores = lambda src, c, tk: _dot_nt(kcat_refs[src][c * tk:(c + 1) * tk, :], q)

    m = jnp.full((1, tq), -jnp.inf, F32)
    acc = jnp.zeros((MLA_VT_ROWS, tq), F32)
    s_next = scores(*steps[0])
    for idx, (src, c, tk) in enumerate(steps):
        s = s_next
        if idx + 1 < len(steps):
            s_next = scores(*steps[idx + 1])
        m_new = jnp.maximum(m, jnp.max(s, axis=0, keepdims=True))
        p = jnp.exp2(s - m_new)
        acc = jnp.exp2(m - m_new) * acc + _dot(rest[3 * src + 2][c], p.astype(BF16))
        m = m_new

    l = acc[MLA_V:MLA_V + 1, :]
    o_ref[...] = (acc[0:MLA_V, :] * (1.0 / l)).T.astype(o_ref.dtype)


def _attn(qn, qr, sources, *, tq, name):
    bsz, t, _ = qn.shape
    in_specs = [pl.BlockSpec((None, tq, HEAD_LANES), lambda b, h, i: (b, i, h))] * 2
    args, chunks, scratch = [qn, qr], [], []
    for kn, kr, vt in sources:
        tk_total, n_chunks, tk = kn.shape[1], vt.shape[1], vt.shape[3]
        in_specs += [pl.BlockSpec((None, tk_total, HEAD_LANES), lambda b, h, i: (b, 0, h)),
                     pl.BlockSpec((None, tk_total, HEAD_LANES), lambda b, h, i: (b, 0, 0)),
                     pl.BlockSpec((None, n_chunks, MLA_VT_ROWS, tk), lambda b, h, i: (b, 0, h, 0))]
        args += [kn, kr, vt]
        chunks.append((n_chunks, tk))
        scratch.append(pltpu.VMEM((tk_total, 2 * HEAD_LANES), BF16))
    return pl.pallas_call(
        functools.partial(_attn_body, chunks=tuple(chunks)),
        grid=(bsz, MLA_HEADS, t // tq),
        in_specs=in_specs,
        out_specs=pl.BlockSpec((None, tq, MLA_V), lambda b, h, i: (b, i, h)),
        out_shape=jax.ShapeDtypeStruct((bsz, t, MLA_HEADS * MLA_V), BF16),
        scratch_shapes=scratch,
        compiler_params=_params(("parallel", "parallel", "arbitrary")),
        name=name,
    )(*args)


def _gla_tables():
    c = GLA_CHUNK
    i = np.arange(c)
    r, m = i[:, None], i[None, :]
    sums = np.zeros((2, GLA_LEVELS + 2, c, c), bool)
    pair = np.zeros((2, GLA_LEVELS + 1, c, c), bool)
    role = np.zeros((2, GLA_LEVELS, c), bool)
    sums[0, 0], sums[1, 0] = m <= r, m >= r
    sums[0, -1], sums[1, -1] = m > r, m < r
    for lv in range(GLA_LEVELS):
        s = c >> lv
        mid = ((i // s) * s + s // 2)[:, None]
        late = ((i % s) >= s // 2)
        same = (i[:, None] // s) == (i[None, :] // s)
        sums[0, 1 + lv] = np.where(late[:, None], (m >= mid) & (m <= r), (m > r) & (m < mid))
        sums[1, 1 + lv] = np.where(late[:, None], (m >= mid) & (m < r), (m >= r) & (m < mid))
        role[0, lv], role[1, lv] = late, ~late
        pair[0, lv] = same & late[:, None] & ~late[None, :]
        pair[1, lv] = same & ~late[:, None] & late[None, :]
    pair[:, GLA_LEVELS] = np.eye(c, dtype=bool)
    sums = sums.reshape(2, (GLA_LEVELS + 2) * c, c)
    sums = np.concatenate([sums, sums], axis=2)
    role = np.broadcast_to(role[..., None], role.shape + (GLA_DK,))
    return (jnp.asarray(sums.astype(np.float32), dtype=BF16), jnp.asarray(pair.astype(np.float32)),
            jnp.asarray(role.astype(np.float32), dtype=BF16))


def _gla_decay_steps(logd_ref, sums_ref, ex_ref, tot_ref, d, slot, r0):
    c = GLA_CHUNK
    p1, p2 = _split2(logd_ref[pl.ds(r0, c), :])
    parts = jnp.concatenate([p1, p2], axis=0)
    cum = None
    for n in range(GLA_LEVELS + 2):
        span = slice(n * c, (n + 1) * c)
        s = c >> max(n - 1, 0)
        if 1 <= n <= GLA_VPU_LEVELS:
            off = s // 2 - 1 if d == 0 else s // 2
            ref = [jnp.broadcast_to(cum[b * s + off:b * s + off + 1, :], (s, GLA_DK))
                   for b in range(c // s)]
            ee = -jnp.abs(cum - (ref[0] if len(ref) == 1 else jnp.concatenate(ref, axis=0)))
        elif n == GLA_LEVELS + 1:
            end = c - 1 if d == 0 else 0
            ee = cum[end:end + 1, :] - cum
        else:
            ee = _dot(sums_ref[d, span, :], parts)
        if n == 0:
            cum = ee
        ex = jnp.exp2(ee)
        ex_ref[d, slot, span, :] = ex.astype(BF16)
        if n == 0:
            total = ex[c - 1:c, :] if d == 0 else ex[0:1, :]
            tot_ref[d, slot] = jnp.broadcast_to(total, tot_ref.shape[2:])
        yield


def _gla_intra_steps(q_ref, k_ref, pair_ref, role_ref, ex_ref, d, slot, r0, box):
    c = GLA_CHUNK
    fwd = d == 0
    rows = pl.ds(r0, c)
    qb, kb = q_ref[rows, :], k_ref[rows, :]
    att = (jnp.sum(qb.astype(F32) * kb.astype(F32), axis=1, keepdims=True)
           * pair_ref[d, GLA_LEVELS])
    yield
    for lv in range(GLA_LEVELS):
        half = c >> (lv + 1)
        if half >= 16:
            x = jnp.concatenate([(qb if (b % 2 == 1) == fwd else kb)[b * half:(b + 1) * half, :]
                                 for b in range(c // half)], axis=0)
        else:
            x = jnp.where(role_ref[d, lv] > 0, qb, kb)
        x = x * ex_ref[d, slot, (1 + lv) * c:(2 + lv) * c, :]
        att = att + pair_ref[d, lv] * _dot_nt(x, x)
        yield
    box.append(att.astype(BF16))


def _gla_state_steps(q_ref, k_ref, v_ref, ex_ref, tot_ref, st_ref, d, items, out):
    c = GLA_CHUNK
    for slot, r0, box in items:
        rows = pl.ds(r0, c)
        block = lambda n: ex_ref[d, slot, n * c:(n + 1) * c, :]
        qb, kb, v = q_ref[rows, :], k_ref[rows, :], v_ref[rows, :]
        st = st_ref[...]
        out.append(_dot_nt(qb * block(0), st.astype(BF16)) + _dot(box[0], v))
        yield
        st_ref[...] = st * tot_ref[d, slot, 0:1, :] + _dot_tn(v, kb * block(GLA_LEVELS + 1))
        yield


def _interleave(chains):
    chains = list(chains)
    while chains:
        for ch in list(chains):
            if next(ch, StopIteration) is StopIteration:
                chains.remove(ch)


def _gla_body(q_ref, k_ref, v_ref, r_ref, ldf_ref, ldb_ref, ng_ref,
              s0f_ref, s0b_ref, sums_ref, pair_ref, role_ref,
              y_ref, sf_ref, sb_ref,
              o_ref, stf_ref, stb_ref, ex_ref, tot_ref):
    t = q_ref.shape[0]
    c = GLA_CHUNK
    n = t // c

    stf_ref[...] = s0f_ref[...].T
    stb_ref[...] = s0b_ref[...].T
    logds = (ldf_ref, ldb_ref)
    states = (stf_ref, stb_ref)
    per = ex_ref.shape[1] // 2
    start = lambda d, p: pl.multiple_of((p if d == 0 else n - 1 - p) * c, c)
    decay = lambda d, p: _gla_decay_steps(logds[d], sums_ref, ex_ref, tot_ref, d,
                                          p % (2 * per), start(d, jnp.minimum(p, n - 1)))

    _interleave(decay(d, p) for p in range(per) for d in range(2))

    def step(it):
        pos = [it * per + j for j in range(per)]
        boxes = {(d, j): [] for d in range(2) for j in range(per)}
        _interleave(
            [_gla_intra_steps(q_ref, k_ref, pair_ref, role_ref, ex_ref, d, pos[j] % (2 * per),
                              start(d, pos[j]), boxes[d, j]) for j in range(per) for d in range(2)]
            + [decay(d, p + per) for p in pos for d in range(2)])
        out = ([], [])
        _interleave(
            _gla_state_steps(q_ref, k_ref, v_ref, ex_ref, tot_ref, states[d], d,
                             [(pos[j] % (2 * per), start(d, pos[j]), boxes[d, j])
                              for j in range(per)], out[d]) for d in range(2))
        return [(start(d, pos[j]), out[d][j]) for d in range(2) for j in range(per)]

    def first_touch(it, carry):
        for r0, o in step(it):
            o_ref[pl.ds(r0, c), :] = o
        return carry

    def second_touch(it, carry):
        for r0, o in step(it):
            rows = pl.ds(r0, c)
            o = _rms_rows(o_ref[rows, :] + o, ng_ref[...])
            r = r_ref[rows, :].astype(F32)
            y_ref[rows, :] = (o * (r * jax.nn.sigmoid(r))).astype(y_ref.dtype)
        return carry

    lax.fori_loop(0, n // (2 * per), first_touch, 0)
    lax.fori_loop(n // (2 * per), n // per, second_touch, 0)

    sf_ref[...] = stf_ref[...].T
    sb_ref[...] = stb_ref[...].T


def _decay_body(x_ref, sc_ref, sh_ref, wg_ref, bg_ref, wa_ref, ba_ref, o_ref):
    x = x_ref[...].astype(F32)
    h = (x * (1.0 + sc_ref[...]) + sh_ref[...]).astype(BF16)
    gh, gl = _split2(_dot(h, wg_ref[...]) + bg_ref[...])
    wh, wl = _split2(wa_ref[...])
    z = _dot(gh, wh) + _dot(gh, wl) + _dot(gl, wh) + ba_ref[...]
    o_ref[...] = (jnp.minimum(z, 0.0) - jnp.log1p(jnp.exp(-jnp.abs(z)))) * (LOG2_E / GLA_TAU)


def _decay(x, scale, shift, w, *, name):
    bsz, t, k = x.shape
    tm = min(t, 1024)
    n = w["wa"].shape[1]
    mmap = _mod_map(scale, 2)
    full = lambda a: pl.BlockSpec(a.shape, lambda b, i: (0,) * a.ndim)
    weights = [w["w_gate"], w["b_gate"], w["wa"], w["ba"]]
    return pl.pallas_call(
        _decay_body,
        grid=(bsz, t // tm),
        in_specs=[pl.BlockSpec((None, tm, k), lambda b, i: (b, i, 0)),
                  pl.BlockSpec((None, 1, k), mmap), pl.BlockSpec((None, 1, k), mmap)]
                 + [full(a) for a in weights],
        out_specs=pl.BlockSpec((None, tm, n), lambda b, i: (b, i, 0)),
        out_shape=jax.ShapeDtypeStruct((bsz, t, n), F32),
        compiler_params=_params(("parallel", "parallel")),
        name=name,
    )(x, scale, shift, *weights)


def _gla(qkvr, logd, w, s0f, s0b, tables, *, name):
    bsz, t, _ = qkvr.shape
    assert t % (2 * GLA_CHUNK) == 0
    per = 1
    sums, pair, role = tables
    kv0 = GLA_HEADS * GLA_DK // GLA_DK
    v0 = 2 * GLA_HEADS * GLA_DK // GLA_DV
    r0 = v0 + GLA_HEADS
    col = lambda width, off: pl.BlockSpec((None, t, width), lambda b, h: (b, 0, off + h))
    head = lambda a: pl.BlockSpec((None,) + a.shape[1:], lambda b, h: (h,) + (0,) * (a.ndim - 1))
    full = lambda a: pl.BlockSpec(a.shape, lambda b, h: (0,) * a.ndim)
    state = pl.BlockSpec((None, None, GLA_DK, GLA_DV), lambda b, h: (b, h, 0, 0))
    return pl.pallas_call(
        _gla_body,
        grid=(bsz, GLA_HEADS),
        in_specs=[col(GLA_DK, 0), col(GLA_DK, kv0), col(GLA_DV, v0), col(GLA_DV, r0),
                  col(GLA_DK, 0), col(GLA_DK, GLA_HEADS), head(w["ng"]),
                  state, state, full(sums), full(pair), full(role)],
        out_specs=[pl.BlockSpec((None, t, GLA_DV), lambda b, h: (b, 0, h)), state, state],
        out_shape=[jax.ShapeDtypeStruct((bsz, t, GLA_HEADS * GLA_DV), BF16),
                   jax.ShapeDtypeStruct((bsz, GLA_HEADS, GLA_DK, GLA_DV), F32),
                   jax.ShapeDtypeStruct((bsz, GLA_HEADS, GLA_DK, GLA_DV), F32)],
        scratch_shapes=[pltpu.VMEM((t, GLA_DV), F32),
                        pltpu.VMEM((GLA_DV, GLA_DK), F32), pltpu.VMEM((GLA_DV, GLA_DK), F32),
                        pltpu.VMEM((2, 2 * per, (GLA_LEVELS + 2) * GLA_CHUNK, GLA_DK), BF16),
                        pltpu.VMEM((2, 2 * per, 8, GLA_DK), F32)],
        compiler_params=_params(("parallel", "parallel")),
        name=name,
    )(qkvr, qkvr, qkvr, qkvr, logd, logd, w["ng"], s0f, s0b, sums, pair, role)


def _conv_shift_table():
    r = np.arange(CONV_WINDOW)
    return jnp.asarray(np.stack([(r[None, :] == r[:, None] + ph) for ph in range(8)])
                       .astype(np.float32), dtype=BF16)


def _conv_body(xm_ref, xp_ref, xn_ref, shift_ref, dw_ref, db_ref, lg_ref, lb_ref, o_ref,
               buf_ref, win_ref, y_ref):
    tt = xm_ref.shape[0]
    i = pl.program_id(1)
    keep_p = jnp.where(i == 0, 0.0, 1.0).astype(BF16)
    keep_n = jnp.where(i == pl.num_programs(1) - 1, 0.0, 1.0).astype(BF16)
    buf_ref[0:CONV_HALO, :] = xp_ref[...] * keep_p
    buf_ref[CONV_HALO:CONV_HALO + tt, :] = xm_ref[...]
    buf_ref[CONV_HALO + tt:, :] = xn_ref[...] * keep_n

    def row_block(rb, carry):
        r0 = pl.multiple_of(rb * CONV_ROWS, CONV_ROWS)
        window = buf_ref[pl.ds(r0, CONV_WINDOW), :]
        for ph in range(8):
            win_ref[ph] = _dot(shift_ref[ph], window)
        for rs in range(CONV_ROWS // CONV_SUB):
            for cc in range(D_MODEL // CONV_LANES):
                lanes = slice(cc * CONV_LANES, (cc + 1) * CONV_LANES)
                acc = jnp.zeros((CONV_SUB, CONV_LANES), F32) + db_ref[:, lanes]
                for tap in range(CONV_W):
                    a, ph = divmod(tap + 1, 8)
                    lo = rs * CONV_SUB + 8 * a
                    acc = acc + dw_ref[tap:tap + 1, lanes] * win_ref[ph, lo:lo + CONV_SUB, lanes]
                y_ref[pl.ds(r0 + rs * CONV_SUB, CONV_SUB), lanes] = acc
        return carry

    lax.fori_loop(0, tt // CONV_ROWS, row_block, 0)
    z = _layer_norm_rows(y_ref[...], lg_ref[...], lb_ref[...])
    o_ref[...] = (z * jax.nn.sigmoid(z)).astype(o_ref.dtype)


def _conv(glu, dw, db, lg, lb, *, name):
    bsz, t, ch = glu.shape
    tt = min(t, 512)
    nh = tt // CONV_HALO
    n_halo = t // CONV_HALO
    shift = _conv_shift_table()
    full = lambda a: pl.BlockSpec(a.shape, lambda b, i: (0,) * a.ndim)
    return pl.pallas_call(
        _conv_body,
        grid=(bsz, t // tt),
        in_specs=[pl.BlockSpec((None, tt, ch), lambda b, i: (b, i, 0)),
                  pl.BlockSpec((None, CONV_HALO, ch),
                               lambda b, i: (b, jnp.maximum(i * nh - 1, 0), 0)),
                  pl.BlockSpec((None, CONV_HALO, ch),
                               lambda b, i: (b, jnp.minimum((i + 1) * nh, n_halo - 1), 0)),
                  full(shift), full(dw), full(db), full(lg), full(lb)],
        out_specs=pl.BlockSpec((None, tt, ch), lambda b, i: (b, i, 0)),
        out_shape=jax.ShapeDtypeStruct((bsz, t, ch), BF16),
        scratch_shapes=[pltpu.VMEM((tt + 2 * CONV_HALO, ch), BF16),
                        pltpu.VMEM((8, CONV_WINDOW, ch), F32), pltpu.VMEM((tt, ch), F32)],
        compiler_params=_params(("parallel", "parallel")),
        name=name,
    )(glu, glu, glu, shift, dw, db, lg, lb)


def _merge_body(pa_ref, pb_ref, pc_ref, gt_ref, x_ref, g1_ref, wa, wb, wc, wo, lg_ref, lb_ref,
                o_ref, *, alpha):
    d = D_MODEL
    m = gt_ref[:, 0:d].astype(F32) * _dot(pa_ref[...], wa[...])
    m = m + gt_ref[:, d:2 * d].astype(F32) * _dot(pb_ref[...], wb[...])
    m = m + gt_ref[:, 2 * d:3 * d].astype(F32) * _dot(pc_ref[...], wc[...])
    y = _dot(m.astype(BF16), wo[...])
    z = alpha * x_ref[...] + g1_ref[...] * y
    o_ref[...] = _layer_norm_rows(z, lg_ref[...], lb_ref[...])


def _merge(pa, pb, pc, gates, x, g1, w, *, alpha, name):
    bsz, t, d = x.shape
    tm = min(t, 512)
    row = lambda width: pl.BlockSpec((None, tm, width), lambda b, i: (b, i, 0))
    full = lambda a: pl.BlockSpec(a.shape, lambda b, i: (0,) * a.ndim,
                                  pipeline_mode=pl.Buffered(1))
    weights = [w["gla_wo"], w["mla_wo"], w["conv_wo"], w["w_out"], w["ln1_g"], w["ln1_b"]]
    return pl.pallas_call(
        functools.partial(_merge_body, alpha=alpha),
        grid=(bsz, t // tm),
        in_specs=[row(d), row(d), row(d), row(3 * d), row(d),
                  pl.BlockSpec((None, 1, d), _mod_map(g1, 2))] + [full(a) for a in weights],
        out_specs=row(d),
        out_shape=jax.ShapeDtypeStruct((bsz, t, d), F32),
        compiler_params=_params(("parallel", "parallel")),
        name=name,
    )(pa, pb, pc, gates, x, g1, *weights)


def _ffn_body(xm_ref, xp_ref, xn_ref, sc_ref, sh_ref, g2_ref, wup_ref, dw_ref, db_ref, wd_ref,
              lg_ref, lb_ref, o_ref, *, alpha):
    tm = xm_ref.shape[0]
    i = pl.program_id(1)
    rows = tm + 2 * FFN_HALO
    mid = slice(FFN_HALO, FFN_HALO + tm)

    mod = lambda x: x * (1.0 + sc_ref[...]) + sh_ref[...]
    keep_p = jnp.where(i == 0, 0.0, 1.0)
    keep_n = jnp.where(i == pl.num_programs(1) - 1, 0.0, 1.0)
    h = jnp.concatenate([(mod(xp_ref[...]) * keep_p).astype(BF16),
                         mod(xm_ref[...]).astype(BF16),
                         (mod(xn_ref[...]) * keep_n).astype(BF16)], axis=0)

    n_chunks = D_FF // FFN_CHUNK
    cols = lambda c: (slice(c * FFN_CHUNK, (c + 1) * FFN_CHUNK),
                      slice(D_FF + c * FFN_CHUNK, D_FF + (c + 1) * FFN_CHUNK))
    up = lambda c: tuple(_dot(h, wup_ref[:, col]) for col in cols(c))

    def conv3(u, col):
        prev = pltpu.roll(u, 1, 0)[mid, :]
        nxt = pltpu.roll(u, rows - 1, 0)[mid, :]
        return (dw_ref[0:1, col] * prev + dw_ref[1:2, col] * u[mid, :] + dw_ref[2:3, col] * nxt
                + db_ref[:, col])

    y = None
    u_next = up(0)
    for c in range(n_chunks):
        ug, uv = u_next
        if c + 1 < n_chunks:
            u_next = up(c + 1)
        cg, cv = conv3(ug, cols(c)[0]), conv3(uv, cols(c)[1])
        a = (cg * jax.nn.sigmoid(cg) * cv).astype(BF16)
        part = _dot(a, wd_ref[c * FFN_CHUNK:(c + 1) * FFN_CHUNK, :])
        y = part if y is None else y + part

    z = alpha * xm_ref[...] + g2_ref[...] * y
    o_ref[...] = _layer_norm_rows(z, lg_ref[...], lb_ref[...])


def _ffn(x, scale, shift, g2, w, *, alpha, name):
    bsz, t, d = x.shape
    tm = min(t, 1024)
    nh = tm // FFN_HALO
    n_halo = t // FFN_HALO
    mmap = _mod_map(scale, 2)
    resident = lambda a: pl.BlockSpec(a.shape, lambda b, i: (0,) * a.ndim,
                                      pipeline_mode=pl.Buffered(1))
    weights = [w["ffn_wup"], w["ffn_dw"], w["ffn_db"], w["ffn_wdown"], w["ln2_g"], w["ln2_b"]]
    return pl.pallas_call(
        functools.partial(_ffn_body, alpha=alpha),
        grid=(bsz, t // tm),
        in_specs=[pl.BlockSpec((None, tm, d), lambda b, i: (b, i, 0)),
                  pl.BlockSpec((None, FFN_HALO, d),
                               lambda b, i: (b, jnp.maximum(i * nh - 1, 0), 0)),
                  pl.BlockSpec((None, FFN_HALO, d),
                               lambda b, i: (b, jnp.minimum((i + 1) * nh, n_halo - 1), 0)),
                  pl.BlockSpec((None, 1, d), mmap), pl.BlockSpec((None, 1, d), mmap),
                  pl.BlockSpec((None, 1, d), mmap)] + [resident(a) for a in weights],
        out_specs=pl.BlockSpec((None, tm, d), lambda b, i: (b, i, 0)),
        out_shape=jax.ShapeDtypeStruct((bsz, t, d), F32),
        compiler_params=_params(("parallel", "parallel")),
        name=name,
    )(x, x, x, scale, shift, g2, *weights)


def _rope_tables(t):
    rows = t // GRID_W
    row = jnp.repeat(jnp.arange(rows, dtype=F32), GRID_W)
    colv = jnp.tile(jnp.arange(GRID_W, dtype=F32), rows)
    inv = ROPE_BASE ** (-2.0 * jnp.arange(ROPE_F, dtype=F32) / (MLA_ROPE // 2))
    ang = jnp.concatenate([row[:, None] * inv, row[:, None] * inv,
                           colv[:, None] * inv, colv[:, None] * inv], axis=1)
    cos, sin = jnp.cos(ang), jnp.sin(ang)
    lane = jnp.arange(MLA_ROPE)
    first = (lane % (2 * ROPE_F)) < ROPE_F
    pad = lambda a: jnp.pad(a, ((0, 0), (0, HEAD_LANES - MLA_ROPE)))
    return (pad(cos), pad(jnp.where(first, -sin, 0.0)), pad(jnp.where(first, 0.0, sin)))


def _prep_layer(p):
    d = D_MODEL
    w_in, b_in = p["w_in"], p["b_in"]
    edges = np.cumsum([0, 512, 512, 1024, 1024, 16, 16, MLA_Q_RANK, MLA_KV_RANK, MLA_ROPE, 2 * d, 3 * d])
    col = lambda a, b: (w_in[:, edges[a]:edges[b]].astype(BF16), b_in[edges[a]:edges[b]][None, :])
    out = {}
    qscale = jnp.where(jnp.arange(edges[4]) < edges[1], GLA_DK ** -0.5, 1.0)
    out["w_gla"] = (w_in[:, :edges[4]] * qscale).astype(BF16)
    out["b_gla"] = (b_in[:edges[4]] * qscale)[None, :]
    wg, bg = col(4, 6)
    out["w_gate"] = jnp.pad(wg, ((0, 0), (0, HEAD_LANES - 2 * GLA_RANK)))
    out["b_gate"] = jnp.pad(bg, ((0, 0), (0, HEAD_LANES - 2 * GLA_RANK)))
    out["wcq"], out["bcq"] = col(6, 7)
    out["wckv"], out["bckv"] = col(7, 8)
    wkr, bkr = col(8, 9)
    out["wkr"] = jnp.pad(wkr, ((0, 0), (0, HEAD_LANES - MLA_ROPE)))
    out["bkr"] = jnp.pad(bkr, ((0, 0), (0, HEAD_LANES - MLA_ROPE)))
    wconv, bconv = col(9, 10)
    out["w_glu_a"], out["w_glu_g"] = wconv[:, :d], wconv[:, d:]
    out["b_glu_a"], out["b_glu_g"] = bconv[:, :d], bconv[:, d:]
    out["w_mg"], out["b_mg"] = col(10, 11)

    n_dec = GLA_HEADS * GLA_DK
    out["wa"] = jnp.zeros((HEAD_LANES, 2 * n_dec), F32)
    out["wa"] = out["wa"].at[:GLA_RANK, :n_dec].set(p["gla_wa_f"])
    out["wa"] = out["wa"].at[GLA_RANK:2 * GLA_RANK, n_dec:].set(p["gla_wa_b"])
    out["ba"] = jnp.concatenate([p["gla_ba_f"], p["gla_ba_b"]])[None, :]
    out["ng"] = p["gla_norm_g"].reshape(GLA_HEADS, 1, GLA_DV)

    out["gq"] = p["mla_q_norm"][None, :]
    out["gkv"] = p["mla_kv_norm"][None, :]
    wuq = p["mla_wuq"].reshape(MLA_Q_RANK, MLA_HEADS, MLA_NOPE + MLA_ROPE)
    out["wuqn"] = wuq[:, :, :MLA_NOPE].reshape(MLA_Q_RANK, -1).astype(BF16)
    out["wuqr"] = jnp.pad(wuq[:, :, MLA_NOPE:], ((0, 0), (0, 0), (0, HEAD_LANES - MLA_ROPE))
                          ).reshape(MLA_Q_RANK, -1).astype(BF16)
    wukv = p["mla_wukv"].reshape(MLA_KV_RANK, MLA_HEADS, MLA_NOPE + MLA_V)
    out["wukn"] = wukv[:, :, :MLA_NOPE].reshape(MLA_KV_RANK, -1).astype(BF16)
    out["wuvt"] = wukv[:, :, MLA_NOPE:].reshape(MLA_KV_RANK, -1).T.astype(BF16)

    for name in ("gla_wo", "mla_wo", "conv_wo", "w_out", "ffn_wup", "ffn_wdown"):
        out[name] = p[name].astype(BF16)
    for name in ("conv_db", "conv_ln_g", "conv_ln_b", "ln1_g", "ln1_b", "ffn_db", "ln2_g", "ln2_b"):
        out[name] = p[name][None, :]
    out["conv_dw"], out["ffn_dw"] = p["conv_dw"], p["ffn_dw"]
    return out


def _stream(x, mods, w, tabs, gla_tables, states, ctx_keys, *, rope, last_ctx, alpha, tag):
    sh1, sc1, g1, sh2, sc2, g2 = mods
    t = x.shape[1]
    tm = min(t, 512)
    qkvr = _proj(x, sc1, sh1, [w["w_gla"]], [w["b_gla"]], act="none", out_dtype=BF16,
                 tn=1024, name=f"gla_proj_{tag}")
    logd = _decay(x, sc1, sh1, w, name=f"decay_{tag}")
    pre_a, sf, sb = _gla(qkvr, logd, w, states[0], states[1], gla_tables, name=f"gla_{tag}")
    mla = _mla_proj(x, sc1, sh1, w, tabs, rope=rope, want_q=not last_ctx, tm=tm,
                    name=f"mla_proj_{tag}")
    keys = tuple(mla[-3:])
    if last_ctx:
        return None, (sf, sb), keys
    qn, qr = mla[0], mla[1]
    pre_b = _attn(qn, qr, [keys] + ([ctx_keys] if ctx_keys is not None else []), tq=min(t, 2048),
                  name=f"attn_{tag}")
    glu = _proj(x, sc1, sh1, [w["w_glu_a"], w["w_glu_g"]], [w["b_glu_a"], w["b_glu_g"]],
                act="glu", out_dtype=BF16, tn=512, name=f"glu_proj_{tag}")
    pre_c = _conv(glu, w["conv_dw"], w["conv_db"], w["conv_ln_g"], w["conv_ln_b"],
                  name=f"conv_{tag}")
    mg = _proj(x, sc1, sh1, [w["w_mg"]], [w["b_mg"]], act="sigmoid", out_dtype=BF16,
               tn=1024, name=f"merge_gate_{tag}")
    x1 = _merge(pre_a, pre_b, pre_c, mg, x, g1, w, alpha=alpha, name=f"merge_{tag}")
    x2 = _ffn(x1, sc2, sh2, g2, w, alpha=alpha, name=f"ffn_{tag}")
    return x2, (sf, sb), keys


def kernel(x, c, ctx, c_ctx, w_ada, b_ada, w_in, b_in, gla_wa_f, gla_ba_f, gla_wa_b, gla_ba_b,
           gla_norm_g, gla_wo, mla_q_norm, mla_kv_norm, mla_wuq, mla_wukv, mla_wo, conv_dw,
           conv_db, conv_ln_g, conv_ln_b, conv_wo, w_out, ln1_g, ln1_b, ffn_wup, ffn_dw, ffn_db,
           ffn_wdown, ln2_g, ln2_b):
    stacked = dict(w_in=w_in, b_in=b_in, gla_wa_f=gla_wa_f, gla_ba_f=gla_ba_f, gla_wa_b=gla_wa_b,
                   gla_ba_b=gla_ba_b, gla_norm_g=gla_norm_g, gla_wo=gla_wo, mla_q_norm=mla_q_norm,
                   mla_kv_norm=mla_kv_norm, mla_wuq=mla_wuq, mla_wukv=mla_wukv, mla_wo=mla_wo,
                   conv_dw=conv_dw, conv_db=conv_db, conv_ln_g=conv_ln_g, conv_ln_b=conv_ln_b,
                   conv_wo=conv_wo, w_out=w_out, ln1_g=ln1_g, ln1_b=ln1_b, ffn_wup=ffn_wup,
                   ffn_dw=ffn_dw, ffn_db=ffn_db, ffn_wdown=ffn_wdown, ln2_g=ln2_g, ln2_b=ln2_b)
    depth = w_in.shape[0]
    bsz, t, d = x.shape
    alpha = float((2.0 * depth) ** 0.25)

    mod_rows = 16
    c_rows = jnp.zeros((mod_rows, d), F32).at[:bsz].set(c).at[bsz].set(c_ctx)
    mods = _ada(c_rows, w_ada, b_ada)

    tabs = _rope_tables(t)
    ctx_tabs = tuple(a[:ctx.shape[1]] for a in tabs)
    gla_tables = _gla_tables()
    zero_state = jnp.zeros((bsz, GLA_HEADS, GLA_DK, GLA_DV), F32)

    xc = ctx
    for l in range(depth):
        w = _prep_layer({k: v[l] for k, v in stacked.items()})
        lat_mods = [m[:, None, :] for m in jnp.split(mods[l, :bsz], 6, axis=-1)]
        ctx_mods = [m[:, None, :] for m in jnp.split(mods[l, bsz:bsz + 1], 6, axis=-1)]
        last = l == depth - 1
        xc, ctx_states, ctx_keys = _stream(xc, ctx_mods, w, ctx_tabs, gla_tables,
                                           (zero_state, zero_state), None, rope=False,
                                           last_ctx=last, alpha=alpha, tag="ctx")
        x, _, _ = _stream(x, lat_mods, w, tabs, gla_tables, ctx_states, ctx_keys, rope=True,
                          last_ctx=False, alpha=alpha, tag="lat")
    return x
```

```python
import functools

import numpy as np
import jax
import jax.numpy as jnp
from jax import lax
from jax.experimental import pallas as pl
from jax.experimental.pallas import tpu as pltpu

F32 = jnp.float32
BF16 = jnp.bfloat16

D_MODEL = 1024
GRID_W = 64

GLA_HEADS = 4
GLA_DK = 128
GLA_DV = 256
GLA_RANK = 16
GLA_TAU = 16.0
GLA_CHUNK = 128
GLA_LEVELS = 7
GLA_VPU_LEVELS = 1
LOG2_E = 1.4426950408889634

MLA_HEADS = 8
MLA_Q_RANK = 384
MLA_KV_RANK = 256
MLA_NOPE = 128
MLA_ROPE = 64
MLA_V = 128
MLA_SCALE = (MLA_NOPE + MLA_ROPE) ** -0.5
MLA_QSCALE = MLA_SCALE * LOG2_E
MLA_ONES_ROWS = 16
MLA_VT_ROWS = MLA_V + MLA_ONES_ROWS
ROPE_F = MLA_ROPE // 4
ROPE_BASE = 10000.0
HEAD_LANES = 128

CONV_W = 31
CONV_HALO = 16
CONV_ROWS = 128
CONV_WINDOW = CONV_ROWS + 2 * CONV_HALO
CONV_SUB = 64
CONV_LANES = 256

D_FF = 2816
FFN_CHUNK = 256
FFN_HALO = 16

NORM_EPS = 1e-6

V7X_VMEM_LIMIT_BYTES = 56 * 1024 * 1024


def _params(semantics):
    return pltpu.CompilerParams(dimension_semantics=semantics,
                                vmem_limit_bytes=V7X_VMEM_LIMIT_BYTES)


def _dot(a, b):
    return jnp.dot(a, b, preferred_element_type=F32)


def _dot_nt(a, b):
    return lax.dot_general(a, b, (((1,), (1,)), ((), ())), preferred_element_type=F32)


def _dot_tn(a, b):
    return lax.dot_general(a, b, (((0,), (0,)), ((), ())), preferred_element_type=F32)


def _split2(x):
    hi = x.astype(BF16)
    lo = (x - hi.astype(F32)).astype(BF16)
    return hi, lo


def _layer_norm_rows(z, g, b):
    mu = jnp.mean(z, axis=-1, keepdims=True)
    zc = z - mu
    var = jnp.mean(zc * zc, axis=-1, keepdims=True)
    return zc * lax.rsqrt(var + NORM_EPS) * g + b


def _rms_rows(z, g):
    return z * lax.rsqrt(jnp.mean(z * z, axis=-1, keepdims=True) + NORM_EPS) * g


def _ada_body(c_ref, w_ref, b_ref, o_ref):
    c = c_ref[...]
    s = c * jax.nn.sigmoid(c)
    sh, sl = _split2(s)
    wh, wl = _split2(w_ref[...])
    o_ref[...] = _dot(sh, wh) + _dot(sh, wl) + _dot(sl, wh) + b_ref[...]


def _ada(c_rows, w_ada, b_ada):
    depth, k, n = w_ada.shape
    rows = c_rows.shape[0]
    tn = 512
    return pl.pallas_call(
        _ada_body,
        grid=(depth, n // tn),
        in_specs=[pl.BlockSpec((rows, k), lambda l, j: (0, 0)),
                  pl.BlockSpec((None, k, tn), lambda l, j: (l, 0, j)),
                  pl.BlockSpec((None, 1, tn), lambda l, j: (l, 0, j))],
        out_specs=pl.BlockSpec((None, rows, tn), lambda l, j: (l, 0, j)),
        out_shape=jax.ShapeDtypeStruct((depth, rows, n), F32),
        compiler_params=_params(("parallel", "parallel")),
        name="ada_mod",
    )(c_rows, w_ada, b_ada.reshape(depth, 1, n))


def _proj_body(x_ref, sc_ref, sh_ref, *rest, n_w, act):
    w_refs, b_refs = rest[:n_w], rest[n_w:2 * n_w]
    o_ref, h_ref = rest[2 * n_w], rest[2 * n_w + 1]

    @pl.when(pl.program_id(2) == 0)
    def _():
        x = x_ref[...].astype(F32)
        h_ref[...] = (x * (1.0 + sc_ref[...]) + sh_ref[...]).astype(BF16)

    h = h_ref[...]
    ys = [_dot(h, w[...]) + b[...] for w, b in zip(w_refs, b_refs)]
    if act == "glu":
        y = ys[0] * jax.nn.sigmoid(ys[1])
    elif act == "sigmoid":
        y = jax.nn.sigmoid(ys[0])
    else:
        y = ys[0]
    o_ref[...] = y.astype(o_ref.dtype)


def _mod_map(mod, n_grid):
    batched = mod.shape[0] > 1
    if n_grid == 3:
        return (lambda b, i, j: (b, 0, 0)) if batched else (lambda b, i, j: (0, 0, 0))
    return (lambda b, i: (b, 0, 0)) if batched else (lambda b, i: (0, 0, 0))


def _proj(x, scale, shift, ws, bs, *, act, out_dtype, tn, name):
    bsz, t, k = x.shape
    n = ws[0].shape[1]
    tm = min(t, 1024)
    mmap = _mod_map(scale, 3)
    in_specs = [pl.BlockSpec((None, tm, k), lambda b, i, j: (b, i, 0)),
                pl.BlockSpec((None, 1, k), mmap),
                pl.BlockSpec((None, 1, k), mmap)]
    in_specs += [pl.BlockSpec((k, tn), lambda b, i, j: (0, j)) for _ in ws]
    in_specs += [pl.BlockSpec((1, tn), lambda b, i, j: (0, j)) for _ in bs]
    return pl.pallas_call(
        functools.partial(_proj_body, n_w=len(ws), act=act),
        grid=(bsz, t // tm, n // tn),
        in_specs=in_specs,
        out_specs=pl.BlockSpec((None, tm, tn), lambda b, i, j: (b, i, j)),
        out_shape=jax.ShapeDtypeStruct((bsz, t, n), out_dtype),
        scratch_shapes=[pltpu.VMEM((tm, k), BF16)],
        compiler_params=_params(("parallel", "parallel", "arbitrary")),
        name=name,
    )(x, scale, shift, *ws, *bs)


def _rope(x, cs, sup, sdn):
    return x * cs + pltpu.roll(x, HEAD_LANES - ROPE_F, 1) * sup + pltpu.roll(x, ROPE_F, 1) * sdn


def _mla_proj_body(x_ref, sc_ref, sh_ref, wcq, bcq, wckv, bckv, wkr, bkr, gq, gkv,
                   wuqn, wuqr, wukn, wuvt, cs_ref, sup_ref, sdn_ref, *outs, rope, want_q):
    x = x_ref[...].astype(F32)
    h = (x * (1.0 + sc_ref[...]) + sh_ref[...]).astype(BF16)
    if want_q:
        qn_o, qr_o, kn_o, kr_o, vt_o = outs
    else:
        kn_o, kr_o, vt_o = outs

    ckv = _rms_rows(_dot(h, wckv[...]) + bckv[...], gkv[...]).astype(BF16)
    kn_o[...] = _dot(ckv, wukn[...]).astype(BF16)
    vt = _dot_nt(wuvt[...], ckv).astype(BF16)
    ones = jnp.ones((MLA_ONES_ROWS, vt.shape[1]), BF16)
    for hd in range(MLA_HEADS):
        vt_o[hd * MLA_VT_ROWS:hd * MLA_VT_ROWS + MLA_V, :] = vt[hd * MLA_V:(hd + 1) * MLA_V, :]
        vt_o[hd * MLA_VT_ROWS + MLA_V:(hd + 1) * MLA_VT_ROWS, :] = ones
    kr = _dot(h, wkr[...]) + bkr[...]
    if rope:
        kr = _rope(kr, cs_ref[...], sup_ref[...], sdn_ref[...])
    kr_o[...] = kr.astype(BF16)

    if want_q:
        cq = _rms_rows(_dot(h, wcq[...]) + bcq[...], gq[...]).astype(BF16)
        qn_o[...] = (_dot(cq, wuqn[...]) * MLA_QSCALE).astype(BF16)
        qr = _dot(cq, wuqr[...]) * MLA_QSCALE
        for hd in range(MLA_HEADS):
            sl = slice(hd * HEAD_LANES, (hd + 1) * HEAD_LANES)
            qh = qr[:, sl]
            if rope:
                qh = _rope(qh, cs_ref[...], sup_ref[...], sdn_ref[...])
            qr_o[:, sl] = qh.astype(BF16)


def _mla_proj(x, scale, shift, w, tabs, *, rope, want_q, tm, name):
    bsz, t, k = x.shape
    mmap = _mod_map(scale, 2)
    full = lambda a: pl.BlockSpec(a.shape, lambda b, i: (0,) * a.ndim)
    weights = [w["wcq"], w["bcq"], w["wckv"], w["bckv"], w["wkr"], w["bkr"], w["gq"], w["gkv"],
               w["wuqn"], w["wuqr"], w["wukn"], w["wuvt"]]
    hv = MLA_HEADS * HEAD_LANES
    row_spec = lambda width: pl.BlockSpec((None, tm, width), lambda b, i: (b, i, 0))
    out_specs, out_shape = [], []
    if want_q:
        out_specs += [row_spec(hv), row_spec(hv)]
        out_shape += [jax.ShapeDtypeStruct((bsz, t, hv), BF16)] * 2
    out_specs += [row_spec(hv), row_spec(HEAD_LANES),
                  pl.BlockSpec((None, None, MLA_HEADS * MLA_VT_ROWS, tm), lambda b, i: (b, i, 0, 0))]
    out_shape += [jax.ShapeDtypeStruct((bsz, t, hv), BF16),
                  jax.ShapeDtypeStruct((bsz, t, HEAD_LANES), BF16),
                  jax.ShapeDtypeStruct((bsz, t // tm, MLA_HEADS * MLA_VT_ROWS, tm), BF16)]
    tab_spec = pl.BlockSpec((tm, HEAD_LANES), lambda b, i: (i, 0))
    return pl.pallas_call(
        functools.partial(_mla_proj_body, rope=rope, want_q=want_q),
        grid=(bsz, t // tm),
        in_specs=[pl.BlockSpec((None, tm, k), lambda b, i: (b, i, 0)),
                  pl.BlockSpec((None, 1, k), mmap), pl.BlockSpec((None, 1, k), mmap)]
                 + [full(a) for a in weights] + [tab_spec] * 3,
        out_specs=out_specs,
        out_shape=out_shape,
        compiler_params=_params(("parallel", "parallel")),
        name=name,
    )(x, scale, shift, *weights, *tabs)


def _attn_body(qn_ref, qr_ref, *rest, chunks):
    n_src = len(chunks)
    o_ref = rest[3 * n_src]
    kcat_refs = rest[3 * n_src + 1:]

    @pl.when(pl.program_id(2) == 0)
    def _():
        for src in range(n_src):
            kcat_refs[src][:, 0:HEAD_LANES] = rest[3 * src][...]
            kcat_refs[src][:, HEAD_LANES:] = rest[3 * src + 1][...]

    q = jnp.concatenate([qn_ref[...], qr_ref[...]], axis=-1)
    tq = q.shape[0]
    steps = [(src, c, tk) for src, (n_chunks, tk) in enumerate(chunks) for c in range(n_chunks)]
    scores = lambda src, c, tk: _dot_nt(kcat_refs[src][c * tk:(c + 1) * tk, :], q)

    m = jnp.full((1, tq), -jnp.inf, F32)
    acc = jnp.zeros((MLA_VT_ROWS, tq), F32)
    s_next = scores(*steps[0])
    for idx, (src, c, tk) in enumerate(steps):
        s = s_next
        if idx + 1 < len(steps):
            s_next = scores(*steps[idx + 1])
        m_new = jnp.maximum(m, jnp.max(s, axis=0, keepdims=True))
        p = jnp.exp2(s - m_new)
        acc = jnp.exp2(m - m_new) * acc + _dot(rest[3 * src + 2][c], p.astype(BF16))
        m = m_new

    l = acc[MLA_V:MLA_V + 1, :]
    o_ref[...] = (acc[0:MLA_V, :] * (1.0 / l)).T.astype(o_ref.dtype)


def _attn(qn, qr, sources, *, tq, name):
    bsz, t, _ = qn.shape
    in_specs = [pl.BlockSpec((None, tq, HEAD_LANES), lambda b, h, i: (b, i, h))] * 2
    args, chunks, scratch = [qn, qr], [], []
    for kn, kr, vt in sources:
        tk_total, n_chunks, tk = kn.shape[1], vt.shape[1], vt.shape[3]
        in_specs += [pl.BlockSpec((None, tk_total, HEAD_LANES), lambda b, h, i: (b, 0, h)),
                     pl.BlockSpec((None, tk_total, HEAD_LANES), lambda b, h, i: (b, 0, 0)),
                     pl.BlockSpec((None, n_chunks, MLA_VT_ROWS, tk), lambda b, h, i: (b, 0, h, 0))]
        args += [kn, kr, vt]
        chunks.append((n_chunks, tk))
        scratch.append(pltpu.VMEM((tk_total, 2 * HEAD_LANES), BF16))
    return pl.pallas_call(
        functools.partial(_attn_body, chunks=tuple(chunks)),
        grid=(bsz, MLA_HEADS, t // tq),
        in_specs=in_specs,
        out_specs=pl.BlockSpec((None, tq, MLA_V), lambda b, h, i: (b, i, h)),
        out_shape=jax.ShapeDtypeStruct((bsz, t, MLA_HEADS * MLA_V), BF16),
        scratch_shapes=scratch,
        compiler_params=_params(("parallel", "parallel", "arbitrary")),
        name=name,
    )(*args)


def _gla_tables():
    c = GLA_CHUNK
    i = np.arange(c)
    r, m = i[:, None], i[None, :]
    sums = np.zeros((2, GLA_LEVELS + 2, c, c), bool)
    pair = np.zeros((2, GLA_LEVELS + 1, c, c), bool)
    role = np.zeros((2, GLA_LEVELS, c), bool)
    sums[0, 0], sums[1, 0] = m <= r, m >= r
    sums[0, -1], sums[1, -1] = m > r, m < r
    for lv in range(GLA_LEVELS):
        s = c >> lv
        mid = ((i // s) * s + s // 2)[:, None]
        late = ((i % s) >= s // 2)
        same = (i[:, None] // s) == (i[None, :] // s)
        sums[0, 1 + lv] = np.where(late[:, None], (m >= mid) & (m <= r), (m > r) & (m < mid))
        sums[1, 1 + lv] = np.where(late[:, None], (m >= mid) & (m < r), (m >= r) & (m < mid))
        role[0, lv], role[1, lv] = late, ~late
        pair[0, lv] = same & late[:, None] & ~late[None, :]
        pair[1, lv] = same & ~late[:, None] & late[None, :]
    pair[:, GLA_LEVELS] = np.eye(c, dtype=bool)
    sums = sums.reshape(2, (GLA_LEVELS + 2) * c, c)
    sums = np.concatenate([sums, sums], axis=2)
    role = np.broadcast_to(role[..., None], role.shape + (GLA_DK,))
    return (jnp.asarray(sums.astype(np.float32), dtype=BF16), jnp.asarray(pair.astype(np.float32)),
            jnp.asarray(role.astype(np.float32), dtype=BF16))


def _gla_decay_steps(logd_ref, sums_ref, ex_ref, tot_ref, d, slot, r0):
    c = GLA_CHUNK
    p1, p2 = _split2(logd_ref[pl.ds(r0, c), :])
    parts = jnp.concatenate([p1, p2], axis=0)
    cum = None
    for n in range(GLA_LEVELS + 2):
        span = slice(n * c, (n + 1) * c)
        s = c >> max(n - 1, 0)
        if 1 <= n <= GLA_VPU_LEVELS:
            off = s // 2 - 1 if d == 0 else s // 2
            ref = [jnp.broadcast_to(cum[b * s + off:b * s + off + 1, :], (s, GLA_DK))
                   for b in range(c // s)]
            ee = -jnp.abs(cum - (ref[0] if len(ref) == 1 else jnp.concatenate(ref, axis=0)))
        elif n == GLA_LEVELS + 1:
            end = c - 1 if d == 0 else 0
            ee = cum[end:end + 1, :] - cum
        else:
            ee = _dot(sums_ref[d, span, :], parts)
        if n == 0:
            cum = ee
        ex = jnp.exp2(ee)
        ex_ref[d, slot, span, :] = ex.astype(BF16)
        if n == 0:
            total = ex[c - 1:c, :] if d == 0 else ex[0:1, :]
            tot_ref[d, slot] = jnp.broadcast_to(total, tot_ref.shape[2:])
        yield


def _gla_intra_steps(q_ref, k_ref, pair_ref, role_ref, ex_ref, d, slot, r0, box):
    c = GLA_CHUNK
    fwd = d == 0
    rows = pl.ds(r0, c)
    qb, kb = q_ref[rows, :], k_ref[rows, :]
    att = (jnp.sum(qb.astype(F32) * kb.astype(F32), axis=1, keepdims=True)
           * pair_ref[d, GLA_LEVELS])
    yield
    for lv in range(GLA_LEVELS):
        half = c >> (lv + 1)
        if half >= 16:
            x = jnp.concatenate([(qb if (b % 2 == 1) == fwd else kb)[b * half:(b + 1) * half, :]
                                 for b in range(c // half)], axis=0)
        else:
            x = jnp.where(role_ref[d, lv] > 0, qb, kb)
        x = x * ex_ref[d, slot, (1 + lv) * c:(2 + lv) * c, :]
        att = att + pair_ref[d, lv] * _dot_nt(x, x)
        yield
    box.append(att.astype(BF16))


def _gla_state_steps(q_ref, k_ref, v_ref, ex_ref, tot_ref, st_ref, d, items, out):
    c = GLA_CHUNK
    for slot, r0, box in items:
        rows = pl.ds(r0, c)
        block = lambda n: ex_ref[d, slot, n * c:(n + 1) * c, :]
        qb, kb, v = q_ref[rows, :], k_ref[rows, :], v_ref[rows, :]
        st = st_ref[...]
        out.append(_dot_nt(qb * block(0), st.astype(BF16)) + _dot(box[0], v))
        yield
        st_ref[...] = st * tot_ref[d, slot, 0:1, :] + _dot_tn(v, kb * block(GLA_LEVELS + 1))
        yield


def _interleave(chains):
    chains = list(chains)
    while chains:
        for ch in list(chains):
            if next(ch, StopIteration) is StopIteration:
                chains.remove(ch)


def _gla_body(q_ref, k_ref, v_ref, r_ref, ldf_ref, ldb_ref, ng_ref,
              s0f_ref, s0b_ref, sums_ref, pair_ref, role_ref,
              y_ref, sf_ref, sb_ref,
              o_ref, stf_ref, stb_ref, ex_ref, tot_ref):
    t = q_ref.shape[0]
    c = GLA_CHUNK
    n = t // c

    stf_ref[...] = s0f_ref[...].T
    stb_ref[...] = s0b_ref[...].T
    logds = (ldf_ref, ldb_ref)
    states = (stf_ref, stb_ref)
    per = ex_ref.shape[1] // 2
    start = lambda d, p: pl.multiple_of((p if d == 0 else n - 1 - p) * c, c)
    decay = lambda d, p: _gla_decay_steps(logds[d], sums_ref, ex_ref, tot_ref, d,
                                          p % (2 * per), start(d, jnp.minimum(p, n - 1)))

    _interleave(decay(d, p) for p in range(per) for d in range(2))

    def step(it):
        pos = [it * per + j for j in range(per)]
        boxes = {(d, j): [] for d in range(2) for j in range(per)}
        _interleave(
            [_gla_intra_steps(q_ref, k_ref, pair_ref, role_ref, ex_ref, d, pos[j] % (2 * per),
                              start(d, pos[j]), boxes[d, j]) for j in range(per) for d in range(2)]
            + [decay(d, p + per) for p in pos for d in range(2)])
        out = ([], [])
        _interleave(
            _gla_state_steps(q_ref, k_ref, v_ref, ex_ref, tot_ref, states[d], d,
                             [(pos[j] % (2 * per), start(d, pos[j]), boxes[d, j])
                              for j in range(per)], out[d]) for d in range(2))
        return [(start(d, pos[j]), out[d][j]) for d in range(2) for j in range(per)]

    def first_touch(it, carry):
        for r0, o in step(it):
            o_ref[pl.ds(r0, c), :] = o
        return carry

    def second_touch(it, carry):
        for r0, o in step(it):
            rows = pl.ds(r0, c)
            o = _rms_rows(o_ref[rows, :] + o, ng_ref[...])
            r = r_ref[rows, :].astype(F32)
            y_ref[rows, :] = (o * (r * jax.nn.sigmoid(r))).astype(y_ref.dtype)
        return carry

    lax.fori_loop(0, n // (2 * per), first_touch, 0)
    lax.fori_loop(n // (2 * per), n // per, second_touch, 0)

    sf_ref[...] = stf_ref[...].T
    sb_ref[...] = stb_ref[...].T


def _decay_body(x_ref, sc_ref, sh_ref, wg_ref, bg_ref, wa_ref, ba_ref, o_ref):
    x = x_ref[...].astype(F32)
    h = (x * (1.0 + sc_ref[...]) + sh_ref[...]).astype(BF16)
    gh, gl = _split2(_dot(h, wg_ref[...]) + bg_ref[...])
    wh, wl = _split2(wa_ref[...])
    z = _dot(gh, wh) + _dot(gh, wl) + _dot(gl, wh) + ba_ref[...]
    o_ref[...] = (jnp.minimum(z, 0.0) - jnp.log1p(jnp.exp(-jnp.abs(z)))) * (LOG2_E / GLA_TAU)


def _decay(x, scale, shift, w, *, name):
    bsz, t, k = x.shape
    tm = min(t, 1024)
    n = w["wa"].shape[1]
    mmap = _mod_map(scale, 2)
    full = lambda a: pl.BlockSpec(a.shape, lambda b, i: (0,) * a.ndim)
    weights = [w["w_gate"], w["b_gate"], w["wa"], w["ba"]]
    return pl.pallas_call(
        _decay_body,
        grid=(bsz, t // tm),
        in_specs=[pl.BlockSpec((None, tm, k), lambda b, i: (b, i, 0)),
                  pl.BlockSpec((None, 1, k), mmap), pl.BlockSpec((None, 1, k), mmap)]
                 + [full(a) for a in weights],
        out_specs=pl.BlockSpec((None, tm, n), lambda b, i: (b, i, 0)),
        out_shape=jax.ShapeDtypeStruct((bsz, t, n), F32),
        compiler_params=_params(("parallel", "parallel")),
        name=name,
    )(x, scale, shift, *weights)


def _gla(qkvr, logd, w, s0f, s0b, tables, *, name):
    bsz, t, _ = qkvr.shape
    assert t % (2 * GLA_CHUNK) == 0
    per = 1
    sums, pair, role = tables
    kv0 = GLA_HEADS * GLA_DK // GLA_DK
    v0 = 2 * GLA_HEADS * GLA_DK // GLA_DV
    r0 = v0 + GLA_HEADS
    col = lambda width, off: pl.BlockSpec((None, t, width), lambda b, h: (b, 0, off + h))
    head = lambda a: pl.BlockSpec((None,) + a.shape[1:], lambda b, h: (h,) + (0,) * (a.ndim - 1))
    full = lambda a: pl.BlockSpec(a.shape, lambda b, h: (0,) * a.ndim)
    state = pl.BlockSpec((None, None, GLA_DK, GLA_DV), lambda b, h: (b, h, 0, 0))
    return pl.pallas_call(
        _gla_body,
        grid=(bsz, GLA_HEADS),
        in_specs=[col(GLA_DK, 0), col(GLA_DK, kv0), col(GLA_DV, v0), col(GLA_DV, r0),
                  col(GLA_DK, 0), col(GLA_DK, GLA_HEADS), head(w["ng"]),
                  state, state, full(sums), full(pair), full(role)],
        out_specs=[pl.BlockSpec((None, t, GLA_DV), lambda b, h: (b, 0, h)), state, state],
        out_shape=[jax.ShapeDtypeStruct((bsz, t, GLA_HEADS * GLA_DV), BF16),
                   jax.ShapeDtypeStruct((bsz, GLA_HEADS, GLA_DK, GLA_DV), F32),
                   jax.ShapeDtypeStruct((bsz, GLA_HEADS, GLA_DK, GLA_DV), F32)],
        scratch_shapes=[pltpu.VMEM((t, GLA_DV), F32),
                        pltpu.VMEM((GLA_DV, GLA_DK), F32), pltpu.VMEM((GLA_DV, GLA_DK), F32),
                        pltpu.VMEM((2, 2 * per, (GLA_LEVELS + 2) * GLA_CHUNK, GLA_DK), BF16),
                        pltpu.VMEM((2, 2 * per, 8, GLA_DK), F32)],
        compiler_params=_params(("parallel", "parallel")),
        name=name,
    )(qkvr, qkvr, qkvr, qkvr, logd, logd, w["ng"], s0f, s0b, sums, pair, role)


def _conv_shift_table():
    r = np.arange(CONV_WINDOW)
    return jnp.asarray(np.stack([(r[None, :] == r[:, None] + ph) for ph in range(8)])
                       .astype(np.float32), dtype=BF16)


def _conv_body(xm_ref, xp_ref, xn_ref, shift_ref, dw_ref, db_ref, lg_ref, lb_ref, o_ref,
               buf_ref, win_ref, y_ref):
    tt = xm_ref.shape[0]
    i = pl.program_id(1)
    keep_p = jnp.where(i == 0, 0.0, 1.0).astype(BF16)
    keep_n = jnp.where(i == pl.num_programs(1) - 1, 0.0, 1.0).astype(BF16)
    buf_ref[0:CONV_HALO, :] = xp_ref[...] * keep_p
    buf_ref[CONV_HALO:CONV_HALO + tt, :] = xm_ref[...]
    buf_ref[CONV_HALO + tt:, :] = xn_ref[...] * keep_n

    def row_block(rb, carry):
        r0 = pl.multiple_of(rb * CONV_ROWS, CONV_ROWS)
        window = buf_ref[pl.ds(r0, CONV_WINDOW), :]
        for ph in range(8):
            win_ref[ph] = _dot(shift_ref[ph], window)
        for rs in range(CONV_ROWS // CONV_SUB):
            for cc in range(D_MODEL // CONV_LANES):
                lanes = slice(cc * CONV_LANES, (cc + 1) * CONV_LANES)
                acc = jnp.zeros((CONV_SUB, CONV_LANES), F32) + db_ref[:, lanes]
                for tap in range(CONV_W):
                    a, ph = divmod(tap + 1, 8)
                    lo = rs * CONV_SUB + 8 * a
                    acc = acc + dw_ref[tap:tap + 1, lanes] * win_ref[ph, lo:lo + CONV_SUB, lanes]
                y_ref[pl.ds(r0 + rs * CONV_SUB, CONV_SUB), lanes] = acc
        return carry

    lax.fori_loop(0, tt // CONV_ROWS, row_block, 0)
    z = _layer_norm_rows(y_ref[...], lg_ref[...], lb_ref[...])
    o_ref[...] = (z * jax.nn.sigmoid(z)).astype(o_ref.dtype)


def _conv(glu, dw, db, lg, lb, *, name):
    bsz, t, ch = glu.shape
    tt = min(t, 512)
    nh = tt // CONV_HALO
    n_halo = t // CONV_HALO
    shift = _conv_shift_table()
    full = lambda a: pl.BlockSpec(a.shape, lambda b, i: (0,) * a.ndim)
    return pl.pallas_call(
        _conv_body,
        grid=(bsz, t // tt),
        in_specs=[pl.BlockSpec((None, tt, ch), lambda b, i: (b, i, 0)),
                  pl.BlockSpec((None, CONV_HALO, ch),
                               lambda b, i: (b, jnp.maximum(i * nh - 1, 0), 0)),
                  pl.BlockSpec((None, CONV_HALO, ch),
                               lambda b, i: (b, jnp.minimum((i + 1) * nh, n_halo - 1), 0)),
                  full(shift), full(dw), full(db), full(lg), full(lb)],
        out_specs=pl.BlockSpec((None, tt, ch), lambda b, i: (b, i, 0)),
        out_shape=jax.ShapeDtypeStruct((bsz, t, ch), BF16),
        scratch_shapes=[pltpu.VMEM((tt + 2 * CONV_HALO, ch), BF16),
                        pltpu.VMEM((8, CONV_WINDOW, ch), F32), pltpu.VMEM((tt, ch), F32)],
        compiler_params=_params(("parallel", "parallel")),
        name=name,
    )(glu, glu, glu, shift, dw, db, lg, lb)


def _merge_body(pa_ref, pb_ref, pc_ref, gt_ref, x_ref, g1_ref, wa, wb, wc, wo, lg_ref, lb_ref,
                o_ref, *, alpha):
    d = D_MODEL
    m = gt_ref[:, 0:d].astype(F32) * _dot(pa_ref[...], wa[...])
    m = m + gt_ref[:, d:2 * d].astype(F32) * _dot(pb_ref[...], wb[...])
    m = m + gt_ref[:, 2 * d:3 * d].astype(F32) * _dot(pc_ref[...], wc[...])
    y = _dot(m.astype(BF16), wo[...])
    z = alpha * x_ref[...] + g1_ref[...] * y
    o_ref[...] = _layer_norm_rows(z, lg_ref[...], lb_ref[...])


def _merge(pa, pb, pc, gates, x, g1, w, *, alpha, name):
    bsz, t, d = x.shape
    tm = min(t, 512)
    row = lambda width: pl.BlockSpec((None, tm, width), lambda b, i: (b, i, 0))
    full = lambda a: pl.BlockSpec(a.shape, lambda b, i: (0,) * a.ndim,
                                  pipeline_mode=pl.Buffered(1))
    weights = [w["gla_wo"], w["mla_wo"], w["conv_wo"], w["w_out"], w["ln1_g"], w["ln1_b"]]
    return pl.pallas_call(
        functools.partial(_merge_body, alpha=alpha),
        grid=(bsz, t // tm),
        in_specs=[row(d), row(d), row(d), row(3 * d), row(d),
                  pl.BlockSpec((None, 1, d), _mod_map(g1, 2))] + [full(a) for a in weights],
        out_specs=row(d),
        out_shape=jax.ShapeDtypeStruct((bsz, t, d), F32),
        compiler_params=_params(("parallel", "parallel")),
        name=name,
    )(pa, pb, pc, gates, x, g1, *weights)


def _ffn_body(xm_ref, xp_ref, xn_ref, sc_ref, sh_ref, g2_ref, wup_ref, dw_ref, db_ref, wd_ref,
              lg_ref, lb_ref, o_ref, *, alpha):
    tm = xm_ref.shape[0]
    i = pl.program_id(1)
    rows = tm + 2 * FFN_HALO
    mid = slice(FFN_HALO, FFN_HALO + tm)

    mod = lambda x: x * (1.0 + sc_ref[...]) + sh_ref[...]
    keep_p = jnp.where(i == 0, 0.0, 1.0)
    keep_n = jnp.where(i == pl.num_programs(1) - 1, 0.0, 1.0)
    h = jnp.concatenate([(mod(xp_ref[...]) * keep_p).astype(BF16),
                         mod(xm_ref[...]).astype(BF16),
                         (mod(xn_ref[...]) * keep_n).astype(BF16)], axis=0)

    n_chunks = D_FF // FFN_CHUNK
    cols = lambda c: (slice(c * FFN_CHUNK, (c + 1) * FFN_CHUNK),
                      slice(D_FF + c * FFN_CHUNK, D_FF + (c + 1) * FFN_CHUNK))
    up = lambda c: tuple(_dot(h, wup_ref[:, col]) for col in cols(c))

    def conv3(u, col):
        prev = pltpu.roll(u, 1, 0)[mid, :]
        nxt = pltpu.roll(u, rows - 1, 0)[mid, :]
        return (dw_ref[0:1, col] * prev + dw_ref[1:2, col] * u[mid, :] + dw_ref[2:3, col] * nxt
                + db_ref[:, col])

    y = None
    u_next = up(0)
    for c in range(n_chunks):
        ug, uv = u_next
        if c + 1 < n_chunks:
            u_next = up(c + 1)
        cg, cv = conv3(ug, cols(c)[0]), conv3(uv, cols(c)[1])
        a = (cg * jax.nn.sigmoid(cg) * cv).astype(BF16)
        part = _dot(a, wd_ref[c * FFN_CHUNK:(c + 1) * FFN_CHUNK, :])
        y = part if y is None else y + part

    z = alpha * xm_ref[...] + g2_ref[...] * y
    o_ref[...] = _layer_norm_rows(z, lg_ref[...], lb_ref[...])


def _ffn(x, scale, shift, g2, w, *, alpha, name):
    bsz, t, d = x.shape
    tm = min(t, 1024)
    nh = tm // FFN_HALO
    n_halo = t // FFN_HALO
    mmap = _mod_map(scale, 2)
    resident = lambda a: pl.BlockSpec(a.shape, lambda b, i: (0,) * a.ndim,
                                      pipeline_mode=pl.Buffered(1))
    weights = [w["ffn_wup"], w["ffn_dw"], w["ffn_db"], w["ffn_wdown"], w["ln2_g"], w["ln2_b"]]
    return pl.pallas_call(
        functools.partial(_ffn_body, alpha=alpha),
        grid=(bsz, t // tm),
        in_specs=[pl.BlockSpec((None, tm, d), lambda b, i: (b, i, 0)),
                  pl.BlockSpec((None, FFN_HALO, d),
                               lambda b, i: (b, jnp.maximum(i * nh - 1, 0), 0)),
                  pl.BlockSpec((None, FFN_HALO, d),
                               lambda b, i: (b, jnp.minimum((i + 1) * nh, n_halo - 1), 0)),
                  pl.BlockSpec((None, 1, d), mmap), pl.BlockSpec((None, 1, d), mmap),
                  pl.BlockSpec((None, 1, d), mmap)] + [resident(a) for a in weights],
        out_specs=pl.BlockSpec((None, tm, d), lambda b, i: (b, i, 0)),
        out_shape=jax.ShapeDtypeStruct((bsz, t, d), F32),
        compiler_params=_params(("parallel", "parallel")),
        name=name,
    )(x, x, x, scale, shift, g2, *weights)


def _rope_tables(t):
    rows = t // GRID_W
    row = jnp.repeat(jnp.arange(rows, dtype=F32), GRID_W)
    colv = jnp.tile(jnp.arange(GRID_W, dtype=F32), rows)
    inv = ROPE_BASE ** (-2.0 * jnp.arange(ROPE_F, dtype=F32) / (MLA_ROPE // 2))
    ang = jnp.concatenate([row[:, None] * inv, row[:, None] * inv,
                           colv[:, None] * inv, colv[:, None] * inv], axis=1)
    cos, sin = jnp.cos(ang), jnp.sin(ang)
    lane = jnp.arange(MLA_ROPE)
    first = (lane % (2 * ROPE_F)) < ROPE_F
    pad = lambda a: jnp.pad(a, ((0, 0), (0, HEAD_LANES - MLA_ROPE)))
    return (pad(cos), pad(jnp.where(first, -sin, 0.0)), pad(jnp.where(first, 0.0, sin)))


def _prep_layer(p):
    d = D_MODEL
    w_in, b_in = p["w_in"], p["b_in"]
    edges = np.cumsum([0, 512, 512, 1024, 1024, 16, 16, MLA_Q_RANK, MLA_KV_RANK, MLA_ROPE, 2 * d, 3 * d])
    col = lambda a, b: (w_in[:, edges[a]:edges[b]].astype(BF16), b_in[edges[a]:edges[b]][None, :])
    out = {}
    qscale = jnp.where(jnp.arange(edges[4]) < edges[1], GLA_DK ** -0.5, 1.0)
    out["w_gla"] = (w_in[:, :edges[4]] * qscale).astype(BF16)
    out["b_gla"] = (b_in[:edges[4]] * qscale)[None, :]
    wg, bg = col(4, 6)
    out["w_gate"] = jnp.pad(wg, ((0, 0), (0, HEAD_LANES - 2 * GLA_RANK)))
    out["b_gate"] = jnp.pad(bg, ((0, 0), (0, HEAD_LANES - 2 * GLA_RANK)))
    out["wcq"], out["bcq"] = col(6, 7)
    out["wckv"], out["bckv"] = col(7, 8)
    wkr, bkr = col(8, 9)
    out["wkr"] = jnp.pad(wkr, ((0, 0), (0, HEAD_LANES - MLA_ROPE)))
    out["bkr"] = jnp.pad(bkr, ((0, 0), (0, HEAD_LANES - MLA_ROPE)))
    wconv, bconv = col(9, 10)
    out["w_glu_a"], out["w_glu_g"] = wconv[:, :d], wconv[:, d:]
    out["b_glu_a"], out["b_glu_g"] = bconv[:, :d], bconv[:, d:]
    out["w_mg"], out["b_mg"] = col(10, 11)

    n_dec = GLA_HEADS * GLA_DK
    out["wa"] = jnp.zeros((HEAD_LANES, 2 * n_dec), F32)
    out["wa"] = out["wa"].at[:GLA_RANK, :n_dec].set(p["gla_wa_f"])
    out["wa"] = out["wa"].at[GLA_RANK:2 * GLA_RANK, n_dec:].set(p["gla_wa_b"])
    out["ba"] = jnp.concatenate([p["gla_ba_f"], p["gla_ba_b"]])[None, :]
    out["ng"] = p["gla_norm_g"].reshape(GLA_HEADS, 1, GLA_DV)

    out["gq"] = p["mla_q_norm"][None, :]
    out["gkv"] = p["mla_kv_norm"][None, :]
    wuq = p["mla_wuq"].reshape(MLA_Q_RANK, MLA_HEADS, MLA_NOPE + MLA_ROPE)
    out["wuqn"] = wuq[:, :, :MLA_NOPE].reshape(MLA_Q_RANK, -1).astype(BF16)
    out["wuqr"] = jnp.pad(wuq[:, :, MLA_NOPE:], ((0, 0), (0, 0), (0, HEAD_LANES - MLA_ROPE))
                          ).reshape(MLA_Q_RANK, -1).astype(BF16)
    wukv = p["mla_wukv"].reshape(MLA_KV_RANK, MLA_HEADS, MLA_NOPE + MLA_V)
    out["wukn"] = wukv[:, :, :MLA_NOPE].reshape(MLA_KV_RANK, -1).astype(BF16)
    out["wuvt"] = wukv[:, :, MLA_NOPE:].reshape(MLA_KV_RANK, -1).T.astype(BF16)

    for name in ("gla_wo", "mla_wo", "conv_wo", "w_out", "ffn_wup", "ffn_wdown"):
        out[name] = p[name].astype(BF16)
    for name in ("conv_db", "conv_ln_g", "conv_ln_b", "ln1_g", "ln1_b", "ffn_db", "ln2_g", "ln2_b"):
        out[name] = p[name][None, :]
    out["conv_dw"], out["ffn_dw"] = p["conv_dw"], p["ffn_dw"]
    return out


def _stream(x, mods, w, tabs, gla_tables, states, ctx_keys, *, rope, last_ctx, alpha, tag):
    sh1, sc1, g1, sh2, sc2, g2 = mods
    t = x.shape[1]
    tm = min(t, 512)
    x_seq = x
    if sc1.shape[0] == 1:
        x = x_seq.reshape(1, -1, x_seq.shape[-1])
    per_seq = lambda a: a.reshape(x_seq.shape[0], t, a.shape[-1])
    qkvr = per_seq(_proj(x, sc1, sh1, [w["w_gla"]], [w["b_gla"]], act="none", out_dtype=BF16,
                         tn=1024, name=f"gla_proj_{tag}"))
    logd = per_seq(_decay(x, sc1, sh1, w, name=f"decay_{tag}"))
    glu = mg = None
    if not last_ctx:
        glu = per_seq(_proj(x, sc1, sh1, [w["w_glu_a"], w["w_glu_g"]],
                            [w["b_glu_a"], w["b_glu_g"]], act="glu", out_dtype=BF16, tn=512,
                            name=f"glu_proj_{tag}"))
        mg = per_seq(_proj(x, sc1, sh1, [w["w_mg"]], [w["b_mg"]], act="sigmoid", out_dtype=BF16,
                           tn=1024, name=f"merge_gate_{tag}"))
    x = x_seq
    pre_a, sf, sb = _gla(qkvr, logd, w, states[0], states[1], gla_tables, name=f"gla_{tag}")
    mla = _mla_proj(x, sc1, sh1, w, tabs, rope=rope, want_q=not last_ctx, tm=tm,
                    name=f"mla_proj_{tag}")
    keys = tuple(mla[-3:])
    if last_ctx:
        return None, (sf, sb), keys
    qn, qr = mla[0], mla[1]
    pre_b = _attn(qn, qr, [keys] + ([ctx_keys] if ctx_keys is not None else []), tq=min(t, 2048),
                  name=f"attn_{tag}")
    pre_c = _conv(glu, w["conv_dw"], w["conv_db"], w["conv_ln_g"], w["conv_ln_b"],
                  name=f"conv_{tag}")
    x1 = _merge(pre_a, pre_b, pre_c, mg, x, g1, w, alpha=alpha, name=f"merge_{tag}")
    x2 = _ffn(x1, sc2, sh2, g2, w, alpha=alpha, name=f"ffn_{tag}")
    return x2, (sf, sb), keys


def kernel(x, c, ctx, c_ctx, w_ada, b_ada, w_in, b_in, gla_wa_f, gla_ba_f, gla_wa_b, gla_ba_b,
           gla_norm_g, gla_wo, mla_q_norm, mla_kv_norm, mla_wuq, mla_wukv, mla_wo, conv_dw,
           conv_db, conv_ln_g, conv_ln_b, conv_wo, w_out, ln1_g, ln1_b, ffn_wup, ffn_dw, ffn_db,
           ffn_wdown, ln2_g, ln2_b):
    stacked = dict(w_in=w_in, b_in=b_in, gla_wa_f=gla_wa_f, gla_ba_f=gla_ba_f, gla_wa_b=gla_wa_b,
                   gla_ba_b=gla_ba_b, gla_norm_g=gla_norm_g, gla_wo=gla_wo, mla_q_norm=mla_q_norm,
                   mla_kv_norm=mla_kv_norm, mla_wuq=mla_wuq, mla_wukv=mla_wukv, mla_wo=mla_wo,
                   conv_dw=conv_dw, conv_db=conv_db, conv_ln_g=conv_ln_g, conv_ln_b=conv_ln_b,
                   conv_wo=conv_wo, w_out=w_out, ln1_g=ln1_g, ln1_b=ln1_b, ffn_wup=ffn_wup,
                   ffn_dw=ffn_dw, ffn_db=ffn_db, ffn_wdown=ffn_wdown, ln2_g=ln2_g, ln2_b=ln2_b)
    depth = w_in.shape[0]
    bsz, t, d = x.shape
    alpha = float((2.0 * depth) ** 0.25)

    mod_rows = 16
    c_rows = jnp.zeros((mod_rows, d), F32).at[:bsz].set(c).at[bsz].set(c_ctx)
    mods = _ada(c_rows, w_ada, b_ada)

    tabs = _rope_tables(t)
    ctx_tabs = tuple(a[:ctx.shape[1]] for a in tabs)
    gla_tables = _gla_tables()
    zero_state = jnp.zeros((bsz, GLA_HEADS, GLA_DK, GLA_DV), F32)

    xc = ctx
    for l in range(depth):
        w = _prep_layer({k: v[l] for k, v in stacked.items()})
        lat_mods = [m[:, None, :] for m in jnp.split(mods[l, :bsz], 6, axis=-1)]
        ctx_mods = [m[:, None, :] for m in jnp.split(mods[l, bsz:bsz + 1], 6, axis=-1)]
        last = l == depth - 1
        xc, ctx_states, ctx_keys = _stream(xc, ctx_mods, w, ctx_tabs, gla_tables,
                                           (zero_state, zero_state), None, rope=False,
                                           last_ctx=last, alpha=alpha, tag="ctx")
        x, _, _ = _stream(x, lat_mods, w, tabs, gla_tables, ctx_states, ctx_keys, rope=True,
                          last_ctx=False, alpha=alpha, tag="lat")
    return x
```
